```python
import math
import jax, jax.numpy as jnp
from jax import lax
import numpy as np

D_MODEL = 1024
BATCH = 8
SEQ = 2048
DEPTH = 1
DEC_BATCH = 128
DEC_SEQ = 1
PAST_LEN = 16384
PAGE_SIZE = 128

S5_GROUP = 16
S5_GROUPS = D_MODEL // 32
S5_WIDTH = S5_GROUPS * S5_GROUP
S5_STATE = 64
S5_DT_MIN = 0.001
S5_DT_MAX = 0.1

M2_INNER = D_MODEL
M2_HEADDIM = 64
M2_HEADS = M2_INNER // M2_HEADDIM
M2_GROUPS = 4
M2_HPG = M2_HEADS // M2_GROUPS
M2_STATE = 128
M2_CONV = 4
M2_CHUNK = 128
M2_CONV_DIM = M2_INNER + 2 * M2_GROUPS * M2_STATE
M2_DT_MIN = 0.001
M2_DT_MAX = 0.1

PEER_HEADS = 8
PEER_NKEYS = 128
PEER_TOPK = 16
PEER_DKEY = 256
PEER_DHALF = PEER_DKEY // 2
PEER_EXPERTS = PEER_NKEYS * PEER_NKEYS
PEER_BLOCK = 128

IN_DIM = S5_WIDTH + M2_INNER + M2_CONV_DIM + M2_HEADS + 2 * D_MODEL
NORM_EPS = 1e-6
GNORM_EPS = 1e-5

kernel_name = "hybrid_s5_ssd_peer_decode_step"


def rmsnorm(x, g):
    xf = x.astype(jnp.float32)
    out = xf * lax.rsqrt(jnp.mean(xf * xf, axis=-1, keepdims=True) + NORM_EPS)
    return (out * g.astype(jnp.float32)).astype(x.dtype)


def modulate(h, shift, scale):
    return h * (1 + scale[:, None, :]) + shift[:, None, :]


def _complex_affine_combine(e1, e2):
    a1r, a1i, b1r, b1i = e1
    a2r, a2i, b2r, b2i = e2
    return (a2r * a1r - a2i * a1i,
            a2r * a1i + a2i * a1r,
            a2r * b1r - a2i * b1i + b2r,
            a2r * b1i + a2i * b1r + b2i)


def s5_branch(u, a_re, a_im, log_dt, b_re, b_im, c_re, c_im, d, w_glu, b_glu, s0_re, s0_im):
    f32 = jnp.float32
    bsz, L, _ = u.shape
    ug = u.astype(f32).reshape(bsz, L, S5_GROUPS, S5_GROUP)
    dt = jnp.exp(log_dt.astype(f32))[:, None]
    ar = a_re.astype(f32)
    ai = a_im.astype(f32)
    mag = jnp.exp(dt * ar)
    abar_re = mag * jnp.cos(dt * ai)
    abar_im = mag * jnp.sin(dt * ai)
    den = ar * ar + ai * ai
    nr = abar_re - 1.0
    f_re = (nr * ar + abar_im * ai) / den
    f_im = (abar_im * ar - nr * ai) / den
    br = b_re.astype(f32)
    bi = b_im.astype(f32)
    bbar_re = f_re[..., None] * br - f_im[..., None] * bi
    bbar_im = f_re[..., None] * bi + f_im[..., None] * br
    bu_re = jnp.einsum('gph,blgh->blgp', bbar_re, ug)
    bu_im = jnp.einsum('gph,blgh->blgp', bbar_im, ug)
    if s0_re is not None:
        sr = s0_re.astype(f32)
        si = s0_im.astype(f32)
        bu_re = bu_re.at[:, 0].add(abar_re * sr - abar_im * si)
        bu_im = bu_im.at[:, 0].add(abar_re * si + abar_im * sr)
    shp = bu_re.shape
    elems = (jnp.broadcast_to(abar_re, shp), jnp.broadcast_to(abar_im, shp), bu_re, bu_im)
    _, _, s_re, s_im = lax.associative_scan(_complex_affine_combine, elems, axis=1)
    y = (jnp.einsum('ghp,blgp->blgh', c_re.astype(f32), s_re)
         - jnp.einsum('ghp,blgp->blgh', c_im.astype(f32), s_im)
         + d.astype(f32) * ug)
    y = jax.nn.gelu(y.reshape(bsz, L, S5_WIDTH), approximate=False)
    y = y * jax.nn.sigmoid(y @ w_glu.astype(f32) + b_glu.astype(f32))
    return y.astype(u.dtype), s_re[:, -1].astype(u.dtype), s_im[:, -1].astype(u.dtype)


def causal_conv(xbc, conv_w, conv_b, buf):
    bsz, L, C = xbc.shape
    if buf is None:
        buf = jnp.zeros((bsz, M2_CONV - 1, C), xbc.dtype)
    xp = jnp.concatenate([buf.astype(xbc.dtype), xbc], axis=1)
    out = conv_b + xp[:, 0:L] * conv_w[0]
    for k in range(1, M2_CONV):
        out = out + xp[:, k:k + L] * conv_w[k]
    return jax.nn.silu(out), xp[:, xp.shape[1] - (M2_CONV - 1):]


def ssd(x, dt, a, bm, cm, h0):
    bsz, L = x.shape[0], x.shape[1]
    q = M2_CHUNK if L % M2_CHUNK == 0 else L
    nc = L // q
    x = x.reshape(bsz, nc, q, M2_GROUPS, M2_HPG, M2_HEADDIM)
    dt = dt.reshape(bsz, nc, q, M2_GROUPS, M2_HPG)
    bm = bm.reshape(bsz, nc, q, M2_GROUPS, M2_STATE)
    cm = cm.reshape(bsz, nc, q, M2_GROUPS, M2_STATE)
    cs = jnp.cumsum(dt * a, axis=2)
    seg = cs[:, :, :, None] - cs[:, :, None, :]
    mask = jnp.tril(jnp.ones((q, q), dtype=bool))[:, :, None, None]
    decay = jnp.exp(jnp.where(mask, seg, -jnp.inf))
    xdt = x * dt[..., None]
    cb = jnp.einsum('bcign,bcjgn->bcijg', cm, bm)
    y_diag = jnp.einsum('bcijgr,bcjgrp->bcigrp', cb[..., None] * decay, xdt)
    decay_st = jnp.exp(cs[:, :, -1:] - cs)
    states = jnp.einsum('bcjgn,bcjgr,bcjgrp->bcgrpn', bm, decay_st, xdt)
    chunk_decay = jnp.exp(cs[:, :, -1])

    def step(h, inp):
        dec, st = inp
        return dec[..., None, None] * h + st, h

    h_last, h_prev = lax.scan(step, h0, (jnp.moveaxis(chunk_decay, 1, 0), jnp.moveaxis(states, 1, 0)))
    h_prev = jnp.moveaxis(h_prev, 0, 1)
    y_off = jnp.einsum('bcign,bcgrpn,bcigr->bcigrp', cm, h_prev, jnp.exp(cs))
    y = (y_diag + y_off).reshape(bsz, L, M2_GROUPS, M2_HPG, M2_HEADDIM)
    return y, h_last


def mamba2_branch(z, xbc, dt_raw, conv_w, conv_b, dt_bias, a_log, d_skip, g_norm, conv_buf, h0):
    f32 = jnp.float32
    bsz, L, _ = z.shape
    xbc, new_buf = causal_conv(xbc, conv_w, conv_b, conv_buf)
    xs = xbc[..., :M2_INNER].astype(f32).reshape(bsz, L, M2_GROUPS, M2_HPG, M2_HEADDIM)
    bm = xbc[..., M2_INNER:M2_INNER + M2_GROUPS * M2_STATE].astype(f32).reshape(bsz, L, M2_GROUPS, M2_STATE)
    cm = xbc[..., M2_INNER + M2_GROUPS * M2_STATE:].astype(f32).reshape(bsz, L, M2_GROUPS, M2_STATE)
    dt = jax.nn.softplus(dt_raw.astype(f32) + dt_bias.astype(f32)).reshape(bsz, L, M2_GROUPS, M2_HPG)
    a = -jnp.exp(a_log.astype(f32)).reshape(M2_GROUPS, M2_HPG)
    if h0 is None:
        h0 = jnp.zeros((bsz, M2_GROUPS, M2_HPG, M2_HEADDIM, M2_STATE), f32)
    else:
        h0 = h0.astype(f32).reshape(bsz, M2_GROUPS, M2_HPG, M2_HEADDIM, M2_STATE)
    y, h_last = ssd(xs, dt, a, bm, cm, h0)
    y = y + d_skip.astype(f32).reshape(M2_GROUPS, M2_HPG)[..., None] * xs
    y = y.reshape(bsz, L, M2_INNER) * jax.nn.silu(z.astype(f32))
    yg = y.reshape(bsz, L, M2_GROUPS, M2_INNER // M2_GROUPS)
    yg = yg * lax.rsqrt(jnp.mean(yg * yg, axis=-1, keepdims=True) + GNORM_EPS)
    y = yg.reshape(bsz, L, M2_INNER) * g_norm.astype(f32)
    h_last = h_last.reshape(bsz, M2_HEADS, M2_HEADDIM, M2_STATE)
    return y.astype(z.dtype), new_buf, h_last.astype(z.dtype)


def peer(x2, w_query, sub_keys, expert_down, expert_up):
    f32 = jnp.float32
    bsz, L, D = x2.shape
    T = bsz * L
    nblk = -(-T // PEER_BLOCK)
    xt = x2.reshape(T, D)
    xt = jnp.pad(xt, ((0, nblk * PEER_BLOCK - T), (0, 0))).reshape(nblk, PEER_BLOCK, D)
    keys = sub_keys.astype(f32)

    def block(xb):
        qv = (xb @ w_query).astype(f32).reshape(PEER_BLOCK, PEER_HEADS, 2, PEER_DHALF)
        s = jnp.einsum('thsd,hskd->thsk', qv, keys)
        sv, si = lax.top_k(s, PEER_TOPK)
        cand = sv[:, :, 0, :, None] + sv[:, :, 1, None, :]
        cidx = si[:, :, 0, :, None] * PEER_NKEYS + si[:, :, 1, None, :]
        fv, fi = lax.top_k(cand.reshape(PEER_BLOCK, PEER_HEADS, PEER_TOPK * PEER_TOPK), PEER_TOPK)
        eidx = jnp.take_along_axis(cidx.reshape(PEER_BLOCK, PEER_HEADS, PEER_TOPK * PEER_TOPK), fi, axis=-1)
        g = jax.nn.softmax(fv, axis=-1)
        u = expert_down[eidx]
        act = jax.nn.gelu(jnp.einsum('thkd,td->thk', u, xb).astype(f32), approximate=False) * g
        v = expert_up[eidx]
        return jnp.einsum('thk,thkd->td', act.astype(v.dtype), v)

    out = lax.map(block, xt)
    return out.reshape(nblk * PEER_BLOCK, D)[:T].reshape(bsz, L, D).astype(x2.dtype)


def trunk(x, c, state, w_ada, b_ada, g_norm1, g_norm2, w_in, s5_a_re, s5_a_im, s5_log_dt, s5_b_re, s5_b_im,
          s5_c_re, s5_c_im, s5_d, s5_w_glu, s5_b_glu, w_branch_a, conv_w, conv_b, dt_bias, a_log, d_skip,
          g_ssm_norm, w_branch_b, w_out, w_query, sub_keys, expert_down, expert_up, g_final, w_ada_final,
          b_ada_final):
    sc = jax.nn.silu(c)
    o1 = S5_WIDTH
    o2 = o1 + M2_INNER
    o3 = o2 + M2_CONV_DIM
    o4 = o3 + M2_HEADS
    new_re, new_im, new_ssm, new_conv = [], [], [], []
    for l in range(DEPTH):
        if state is None:
            s0r, s0i, h0, buf = None, None, None, None
        else:
            s0r, s0i, h0, buf = state[0][l], state[1][l], state[2][l], state[3][l]
        mod = sc @ w_ada[l] + b_ada[l]
        sh1, sc1, gt1, sh2, sc2, gt2 = jnp.split(mod, 6, axis=-1)
        h = modulate(rmsnorm(x, g_norm1[l]), sh1, sc1)
        proj = h @ w_in[l]
        u = proj[..., :o1]
        z = proj[..., o1:o2]
        xbc = proj[..., o2:o3]
        dt_raw = proj[..., o3:o4]
        gates = jax.nn.sigmoid(proj[..., o4:])
        ya, sr, si = s5_branch(u, s5_a_re[l], s5_a_im[l], s5_log_dt[l], s5_b_re[l], s5_b_im[l], s5_c_re[l],
                               s5_c_im[l], s5_d[l], s5_w_glu[l], s5_b_glu[l], s0r, s0i)
        yb, nbuf, hl = mamba2_branch(z, xbc, dt_raw, conv_w[l], conv_b[l], dt_bias[l], a_log[l], d_skip[l],
                                     g_ssm_norm[l], buf, h0)
        merged = gates[..., :D_MODEL] * (ya @ w_branch_a[l]) + gates[..., D_MODEL:] * (yb @ w_branch_b[l])
        x = x + gt1[:, None, :] * (merged @ w_out[l])
        h2 = modulate(rmsnorm(x, g_norm2[l]), sh2, sc2)
        x = x + gt2[:, None, :] * peer(h2, w_query[l], sub_keys[l], expert_down[l], expert_up[l])
        new_re.append(sr)
        new_im.append(si)
        new_ssm.append(hl)
        new_conv.append(nbuf)
    shf, scf = jnp.split(sc @ w_ada_final + b_ada_final, 2, axis=-1)
    y = modulate(rmsnorm(x, g_final), shf, scf)
    return y, jnp.stack(new_re), jnp.stack(new_im), jnp.stack(new_ssm), jnp.stack(new_conv)


def setup_inputs(seed: int = 0) -> dict:
    key = jax.random.key(seed)
    ks = iter(jax.random.split(key, 48))
    f32 = jnp.float32
    nrm = lambda shape, s: jax.random.normal(next(ks), shape, f32) * s
    D = D_MODEL
    inp = {}
    inp['x_prompt'] = nrm((BATCH, SEQ, D), 1.0)
    inp['x_sample'] = nrm((DEC_BATCH, DEC_SEQ, D), 1.0)
    inp['state_s5_re'] = nrm((DEPTH, DEC_BATCH, S5_GROUPS, S5_STATE), 0.1)
    inp['state_s5_im'] = nrm((DEPTH, DEC_BATCH, S5_GROUPS, S5_STATE), 0.1)
    inp['state_ssm'] = nrm((DEPTH, DEC_BATCH, M2_HEADS, M2_HEADDIM, M2_STATE), 0.1)
    inp['state_conv'] = nrm((DEPTH, DEC_BATCH, M2_CONV - 1, M2_CONV_DIM), 1.0)
    inp['c_prompt'] = nrm((BATCH, D), 1.0)
    inp['c_sample'] = nrm((DEC_BATCH, D), 1.0)
    inp['w_ada'] = nrm((DEPTH, D, 6 * D), 0.5 * D ** -0.5)
    inp['b_ada'] = nrm((DEPTH, 6 * D), 0.01)
    inp['g_norm1'] = 1.0 + nrm((DEPTH, D), 0.05)
    inp['g_norm2'] = 1.0 + nrm((DEPTH, D), 0.05)
    inp['w_in'] = nrm((DEPTH, D, IN_DIM), D ** -0.5)
    inp['s5_a_re'] = -0.5 + nrm((DEPTH, S5_GROUPS, S5_STATE), 0.01)
    inp['s5_a_im'] = jnp.pi * jnp.arange(S5_STATE, dtype=f32) + nrm((DEPTH, S5_GROUPS, S5_STATE), 0.01)
    inp['s5_log_dt'] = jax.random.uniform(next(ks), (DEPTH, S5_GROUPS), f32, math.log(S5_DT_MIN), math.log(S5_DT_MAX))
    inp['s5_b_re'] = nrm((DEPTH, S5_GROUPS, S5_STATE, S5_GROUP), (2 * S5_GROUP) ** -0.5)
    inp['s5_b_im'] = nrm((DEPTH, S5_GROUPS, S5_STATE, S5_GROUP), (2 * S5_GROUP) ** -0.5)
    inp['s5_c_re'] = nrm((DEPTH, S5_GROUPS, S5_GROUP, S5_STATE), S5_STATE ** -0.5)
    inp['s5_c_im'] = nrm((DEPTH, S5_GROUPS, S5_GROUP, S5_STATE), S5_STATE ** -0.5)
    inp['s5_d'] = nrm((DEPTH, S5_GROUPS, S5_GROUP), 1.0)
    inp['s5_w_glu'] = nrm((DEPTH, S5_WIDTH, S5_WIDTH), S5_WIDTH ** -0.5)
    inp['s5_b_glu'] = nrm((DEPTH, S5_WIDTH), 0.01)
    inp['w_branch_a'] = nrm((DEPTH, S5_WIDTH, D), S5_WIDTH ** -0.5)
    inp['conv_w'] = nrm((DEPTH, M2_CONV, M2_CONV_DIM), M2_CONV ** -0.5)
    inp['conv_b'] = nrm((DEPTH, M2_CONV_DIM), 0.01)
    dt0 = jnp.exp(jax.random.uniform(next(ks), (DEPTH, M2_HEADS), f32, math.log(M2_DT_MIN), math.log(M2_DT_MAX)))
    inp['dt_bias'] = dt0 + jnp.log(-jnp.expm1(-dt0))
    inp['a_log'] = jnp.log(jax.random.uniform(next(ks), (DEPTH, M2_HEADS), f32, 1.0, 16.0))
    inp['d_skip'] = 1.0 + nrm((DEPTH, M2_HEADS), 0.1)
    inp['g_ssm_norm'] = 1.0 + nrm((DEPTH, M2_INNER), 0.05)
    inp['w_branch_b'] = nrm((DEPTH, M2_INNER, D), M2_INNER ** -0.5)
    inp['w_out'] = nrm((DEPTH, D, D), D ** -0.5)
    inp['w_query'] = nrm((DEPTH, D, PEER_HEADS * PEER_DKEY), D ** -0.5)
    inp['sub_keys'] = nrm((DEPTH, PEER_HEADS, 2, PEER_NKEYS, PEER_DHALF), PEER_DHALF ** -0.5)
    inp['expert_down'] = nrm((DEPTH, PEER_EXPERTS, D), D ** -0.5)
    inp['expert_up'] = nrm((DEPTH, PEER_EXPERTS, D), PEER_HEADS ** -0.5)
    inp['g_final'] = 1.0 + nrm((D,), 0.05)
    inp['w_ada_final'] = nrm((D, 2 * D), 0.5 * D ** -0.5)
    inp['b_ada_final'] = nrm((2 * D,), 0.01)
    return inp


def reference(x_prompt, x_sample, state_s5_re, state_s5_im, state_ssm, state_conv, c_prompt, c_sample,
              w_ada, b_ada, g_norm1, g_norm2, w_in, s5_a_re, s5_a_im, s5_log_dt, s5_b_re, s5_b_im, s5_c_re,
              s5_c_im, s5_d, s5_w_glu, s5_b_glu, w_branch_a, conv_w, conv_b, dt_bias, a_log, d_skip, g_ssm_norm,
              w_branch_b, w_out, w_query, sub_keys, expert_down, expert_up, g_final, w_ada_final, b_ada_final):
    weights = (w_ada, b_ada, g_norm1, g_norm2, w_in, s5_a_re, s5_a_im, s5_log_dt, s5_b_re, s5_b_im, s5_c_re,
               s5_c_im, s5_d, s5_w_glu, s5_b_glu, w_branch_a, conv_w, conv_b, dt_bias, a_log, d_skip, g_ssm_norm,
               w_branch_b, w_out, w_query, sub_keys, expert_down, expert_up, g_final, w_ada_final, b_ada_final)
    y_prompt, s5_re_p, s5_im_p, ssm_p, conv_p = trunk(x_prompt, c_prompt, None, *weights)
    y_sample, s5_re_s, s5_im_s, ssm_s, conv_s = trunk(
        x_sample, c_sample, (state_s5_re, state_s5_im, state_ssm, state_conv), *weights)
    return (y_prompt, y_sample, s5_re_p, s5_im_p, ssm_p, conv_p, s5_re_s, s5_im_s, ssm_s, conv_s)
```

```python
import functools
import math

import jax
import jax.numpy as jnp
from jax import lax
from jax.experimental import pallas as pl
from jax.experimental.pallas import tpu as pltpu

F32 = jnp.float32
BF16 = jnp.bfloat16
HIGHEST = lax.Precision.HIGHEST

LANES = 128
NORM_EPS = 1e-6
GNORM_EPS = 1e-5

S5_GROUP = 16
S5_STATE = 64
M2_HEADDIM = 64
M2_GROUPS = 4
M2_STATE = 128
M2_CONV = 4
M2_CHUNK = 128
PEER_HEADS = 8
PEER_NKEYS = 128
PEER_TOPK = 16

VMEM_LIMIT = 56 * 1024 * 1024


def _params(*sem):
    return pltpu.CompilerParams(dimension_semantics=sem, vmem_limit_bytes=VMEM_LIMIT)


def _bdot(a, b):
    return jnp.dot(a.astype(BF16), b.astype(BF16), preferred_element_type=F32)


def _rmsnorm(x, g):
    return x * lax.rsqrt(jnp.mean(x * x, axis=-1, keepdims=True) + NORM_EPS) * g


def _rows(ref):
    v = ref[...]
    return v.reshape(-1, v.shape[-1])


def _gelu(x):
    return 0.5 * x * (1.0 + lax.erf(x * (1.0 / math.sqrt(2.0))))


def _softplus(x):
    return jnp.maximum(x, 0.0) + jnp.log1p(jnp.exp(-jnp.abs(x)))


def _cond_kernel(c_ref, w_ref, b_ref, o_ref):
    sc = jax.nn.silu(c_ref[...])
    o_ref[...] = jnp.dot(sc, w_ref[...], precision=HIGHEST, preferred_element_type=F32) + b_ref[...]


def _cond_proj(c, w, b):
    r, d = c.shape
    n = w.shape[1]
    tn = 1024
    return pl.pallas_call(
        _cond_kernel,
        grid=(n // tn,),
        in_specs=[pl.BlockSpec((r, d), lambda j: (0, 0)),
                  pl.BlockSpec((d, tn), lambda j: (0, j)),
                  pl.BlockSpec((1, tn), lambda j: (0, j))],
        out_specs=pl.BlockSpec((r, tn), lambda j: (0, j)),
        out_shape=jax.ShapeDtypeStruct((r, n), F32),
        compiler_params=_params("arbitrary"),
        name="cond_proj",
    )(c, w, b.reshape(1, n))


class _Mod:
    def __init__(self, mod, d, per_token, tiles_per_batch):
        self.per_token = per_token
        self.tpb = tiles_per_batch
        self.d = d
        self.arr = mod if per_token else mod.reshape(mod.shape[0], 1, mod.shape[1])

    def spec(self, k, tile_of):
        if self.per_token:
            return pl.BlockSpec((self.arr.shape[0], self.d), lambda *g: (0, k))
        return pl.BlockSpec((1, 1, self.d), lambda *g: (tile_of(*g) // self.tpb, 0, k))


def _inproj_kernel(x_ref, g_ref, sh_ref, sc_ref, w_ref, o_ref):
    h = _rmsnorm(x_ref[...], g_ref[...]) * (1.0 + _rows(sc_ref)) + _rows(sh_ref)
    o_ref[...] = _bdot(h, w_ref[...])


def _inproj(x2d, g, mod, w_bf16, tt):
    t, d = x2d.shape
    n = w_bf16.shape[1]
    tn = n // 3
    tile_of = lambda j, i: i
    return pl.pallas_call(
        _inproj_kernel,
        grid=(n // tn, t // tt),
        in_specs=[pl.BlockSpec((tt, d), lambda j, i: (i, 0)),
                  pl.BlockSpec((1, d), lambda j, i: (0, 0)),
                  mod.spec(0, tile_of), mod.spec(1, tile_of),
                  pl.BlockSpec((d, tn), lambda j, i: (0, j))],
        out_specs=pl.BlockSpec((tt, tn), lambda j, i: (i, j)),
        out_shape=jax.ShapeDtypeStruct((t, n), F32),
        compiler_params=_params("arbitrary", "arbitrary"),
        name="inproj",
    )(x2d, g.reshape(1, d), mod.arr, mod.arr, w_bf16)


def _s5_kernel(u_ref, s0r_ref, s0i_ref, ar_ref, ai_ref, bre_ref, bim_ref, cre_ref, cim_ref, d_ref,
               wglu_ref, bglu_ref, wa_ref, o_ref, sr_ref, si_ref, s_ref, carry_ref, *, nb, lc):
    nt = ar_ref.shape[1] // LANES
    c = pl.program_id(0)
    u = _rows(u_ref)
    ub = u.astype(BF16)

    @pl.when(c == 0)
    def _():
        for j in range(nt):
            carry_ref[j] = s0r_ref[:, j * LANES:(j + 1) * LANES]
            carry_ref[nt + j] = s0i_ref[:, j * LANES:(j + 1) * LANES]

    gpt = LANES // S5_GROUP
    spt = LANES // S5_STATE
    for j in range(nt):
        k = (j * spt) // gpt
        uk = ub[:, k * LANES:(k + 1) * LANES]
        s_ref[j] = jnp.dot(uk, bre_ref[j], preferred_element_type=F32)
        s_ref[nt + j] = jnp.dot(uk, bim_ref[j], preferred_element_type=F32)

    def step(t, carry):
        new = []
        for j in range(nt):
            cr, ci = carry[2 * j], carry[2 * j + 1]
            ar = ar_ref[:, j * LANES:(j + 1) * LANES]
            ai = ai_ref[:, j * LANES:(j + 1) * LANES]
            idx = pl.ds(t, nb, stride=lc)
            nr = ar * cr - ai * ci + s_ref[j, idx, :]
            ni = ar * ci + ai * cr + s_ref[nt + j, idx, :]
            s_ref[j, idx, :] = nr
            s_ref[nt + j, idx, :] = ni
            new += [nr, ni]
        return tuple(new)

    init = []
    for j in range(nt):
        init += [carry_ref[j], carry_ref[nt + j]]
    fin = lax.fori_loop(0, lc, step, tuple(init))
    for j in range(nt):
        carry_ref[j] = fin[2 * j]
        carry_ref[nt + j] = fin[2 * j + 1]
        sr_ref[:, j * LANES:(j + 1) * LANES] = fin[2 * j]
        si_ref[:, j * LANES:(j + 1) * LANES] = fin[2 * j + 1]

    tpo = gpt // spt
    ys = []
    for m in range(u.shape[1] // LANES):
        acc = d_ref[:, m * LANES:(m + 1) * LANES] * u[:, m * LANES:(m + 1) * LANES]
        for k in range(tpo):
            j = tpo * m + k
            acc = acc + jnp.dot(s_ref[j].astype(BF16), cre_ref[j], preferred_element_type=F32)
            acc = acc - jnp.dot(s_ref[nt + j].astype(BF16), cim_ref[j], preferred_element_type=F32)
        ys.append(acc)
    y = _gelu(jnp.concatenate(ys, axis=1))
    y = y * jax.nn.sigmoid(_bdot(y, wglu_ref[...]) + bglu_ref[...])
    o_ref[...] = _bdot(y, wa_ref[...]).reshape(o_ref.shape)


def _s5_prep(a_re, a_im, log_dt, b_re, b_im, c_re, c_im):
    g, p = a_re.shape
    h = b_re.shape[-1]
    dt = jnp.exp(log_dt)[:, None]
    mag = jnp.exp(dt * a_re)
    abar_re = mag * jnp.cos(dt * a_im)
    abar_im = mag * jnp.sin(dt * a_im)
    den = a_re * a_re + a_im * a_im
    nr = abar_re - 1.0
    f_re = (nr * a_re + abar_im * a_im) / den
    f_im = (abar_im * a_re - nr * a_im) / den
    bbar_re = f_re[..., None] * b_re - f_im[..., None] * b_im
    bbar_im = f_re[..., None] * b_im + f_im[..., None] * b_re
    eye = jnp.eye(g, dtype=F32)
    gpt = LANES // h
    spt = LANES // p
    nt = g // spt

    def in_blocks(bbar):
        full = jnp.einsum('gph,gk->ghkp', bbar, eye).reshape(g * h, g * p)
        return jnp.stack([full[((j * spt) // gpt) * LANES:((j * spt) // gpt + 1) * LANES,
                               j * LANES:(j + 1) * LANES] for j in range(nt)]).astype(BF16)

    def out_blocks(cm):
        full = jnp.einsum('ghp,gk->gpkh', cm, eye).reshape(g * p, g * h)
        tpo = gpt // spt
        return jnp.stack([full[j * LANES:(j + 1) * LANES, (j // tpo) * LANES:(j // tpo + 1) * LANES]
                          for j in range(nt)]).astype(BF16)

    return (abar_re.reshape(1, g * p), abar_im.reshape(1, g * p), in_blocks(bbar_re), in_blocks(bbar_im),
            out_blocks(c_re), out_blocks(c_im))


def _s5_branch(proj3, col_blk, s0r, s0i, prep, d_flat, w_glu, b_glu, w_a, nb, lc):
    abar_re, abar_im, bre, bim, cre, cim = prep
    b3, l3, _ = proj3.shape
    width = w_glu.shape[0]
    dm = w_a.shape[1]
    ns = abar_re.shape[1]
    nt = ns // LANES
    blk_b = b3
    blk_l = nb * lc // b3
    nchunk = l3 // blk_l
    const2 = lambda c: (0, 0)
    const3 = lambda c: (0, 0, 0)
    kern = functools.partial(_s5_kernel, nb=nb, lc=lc)
    return pl.pallas_call(
        kern,
        grid=(nchunk,),
        in_specs=[pl.BlockSpec((blk_b, blk_l, width), lambda c: (0, c, col_blk)),
                  pl.BlockSpec((nb, ns), const2), pl.BlockSpec((nb, ns), const2),
                  pl.BlockSpec((1, ns), const2), pl.BlockSpec((1, ns), const2),
                  pl.BlockSpec((nt, LANES, LANES), const3), pl.BlockSpec((nt, LANES, LANES), const3),
                  pl.BlockSpec((nt, LANES, LANES), const3), pl.BlockSpec((nt, LANES, LANES), const3),
                  pl.BlockSpec((1, width), const2),
                  pl.BlockSpec((width, width), const2), pl.BlockSpec((1, width), const2),
                  pl.BlockSpec((width, dm), const2)],
        out_specs=[pl.BlockSpec((blk_b, blk_l, dm), lambda c: (0, c, 0)),
                   pl.BlockSpec((nb, ns), const2), pl.BlockSpec((nb, ns), const2)],
        out_shape=[jax.ShapeDtypeStruct((b3, l3, dm), F32),
                   jax.ShapeDtypeStruct((nb, ns), F32), jax.ShapeDtypeStruct((nb, ns), F32)],
        scratch_shapes=[pltpu.VMEM((2 * nt, nb * lc, LANES), F32),
                        pltpu.VMEM((2 * nt, nb, LANES), F32)],
        compiler_params=_params("arbitrary"),
        name="s5_branch",
    )(proj3, s0r, s0i, abar_re, abar_im, bre, bim, cre, cim, d_flat,
      w_glu.astype(BF16), b_glu.reshape(1, width), w_a.astype(BF16))


def _head_expand(nh, width):
    r = lax.broadcasted_iota(jnp.int32, (LANES, width), 0)
    c = lax.broadcasted_iota(jnp.int32, (LANES, width), 1)
    return jnp.where((c // (width // nh) == r) & (r < nh), 1.0, 0.0).astype(F32)


def _gated_norm_proj(y, z, gn, wb):
    y = y * jax.nn.silu(z)
    gw = y.shape[1] // M2_GROUPS
    parts = []
    for g in range(M2_GROUPS):
        yg = y[:, g * gw:(g + 1) * gw]
        parts.append(yg * lax.rsqrt(jnp.mean(yg * yg, axis=-1, keepdims=True) + GNORM_EPS))
    y = jnp.concatenate(parts, axis=1) * gn
    return _bdot(y, wb)


def _ssd_kernel(xbc_ref, z_ref, dtr_ref, cw_ref, cb_ref, dtb_ref, alog_ref, dsk_ref, gn_ref, wb_ref,
                o_ref, hout_ref, conv_ref, xp_ref, h_ref, *, nh):
    q = M2_CHUNK
    c = pl.program_id(1)
    inner = z_ref.shape[-1]
    hd = inner // nh
    hpg = nh // M2_GROUPS
    ns = M2_STATE

    @pl.when(c == 0)
    def _():
        xp_ref[0:8] = jnp.zeros((8, xp_ref.shape[1]), F32)
        h_ref[...] = jnp.zeros(h_ref.shape, F32)

    xp_ref[8:8 + q] = xbc_ref[0]
    acc = cb_ref[...] + xp_ref[5:5 + q] * cw_ref[0:1, :]
    for k in range(1, M2_CONV):
        acc = acc + xp_ref[5 + k:5 + k + q] * cw_ref[k:k + 1, :]
    xbc = jax.nn.silu(acc)
    tail = xp_ref[5 + q:8 + q]
    conv_ref[0] = tail
    xp_ref[5:8] = tail

    xs = xbc[:, :inner]
    bm = xbc[:, inner:inner + M2_GROUPS * ns]
    cm = xbc[:, inner + M2_GROUPS * ns:]

    dt = _softplus(dtr_ref[0] + dtb_ref[...])
    da = dt * (-jnp.exp(alog_ref[...]))
    ii = lax.broadcasted_iota(jnp.int32, (q, q), 0)
    jj = lax.broadcasted_iota(jnp.int32, (q, q), 1)
    lower = ii >= jj
    cs = jnp.dot(jnp.where(lower, 1.0, 0.0).astype(F32), da, precision=HIGHEST, preferred_element_type=F32)
    cst = cs.T
    cs_last = cs[q - 1:q, :]
    expand = _head_expand(nh, inner)
    dt_full = jnp.dot(dt, expand, precision=HIGHEST, preferred_element_type=F32)
    dst_full = jnp.dot(jnp.exp(cs_last - cs), expand, precision=HIGHEST, preferred_element_type=F32)
    ecs_full = jnp.dot(jnp.exp(cs), expand, precision=HIGHEST, preferred_element_type=F32)
    xdt = xs * dt_full
    xwt = (xdt * dst_full).T
    lane = lax.broadcasted_iota(jnp.int32, (q, 2 * hd), 1)

    ys = []
    for g in range(M2_GROUPS):
        bg = bm[:, g * ns:(g + 1) * ns]
        cg = cm[:, g * ns:(g + 1) * ns].astype(BF16)
        cb = lax.dot_general(cg, bg.astype(BF16), (((1,), (1,)), ((), ())), preferred_element_type=F32)
        hprev = h_ref[g * hpg:(g + 1) * hpg].reshape(hpg * hd, ns)
        yoff = lax.dot_general(cg, hprev.astype(BF16), (((1,), (1,)), ((), ())), preferred_element_type=F32)
        yoff = yoff * ecs_full[:, g * hpg * hd:(g + 1) * hpg * hd]
        for pr in range(hpg // 2):
            h0 = g * hpg + 2 * pr
            ms = []
            for h in (h0, h0 + 1):
                seg = cs[:, h:h + 1] - cst[h:h + 1, :]
                ms.append((cb * jnp.exp(jnp.where(lower, seg, -jnp.inf))).astype(BF16))
            xp = xdt[:, h0 * hd:(h0 + 2) * hd]
            rhs = jnp.concatenate([jnp.where(lane < hd, xp, 0.0), jnp.where(lane >= hd, xp, 0.0)], axis=0)
            ydiag = jnp.dot(jnp.concatenate(ms, axis=1), rhs.astype(BF16), preferred_element_type=F32)
            ys.append(ydiag + yoff[:, 2 * pr * hd:(2 * pr + 2) * hd])
        st = jnp.dot(xwt[g * hpg * hd:(g + 1) * hpg * hd, :].astype(BF16), bg.astype(BF16),
                     preferred_element_type=F32)
        for hh in range(hpg):
            h = g * hpg + hh
            dec = jnp.exp(cst[h:h + 1, q - 1:q])
            h_ref[h] = dec * h_ref[h] + st[hh * hd:(hh + 1) * hd]
    hout_ref[0] = h_ref[...]
    y = jnp.concatenate(ys, axis=1) + dsk_ref[...] * xs
    o_ref[0] = _gated_norm_proj(y, z_ref[0], gn_ref[...], wb_ref[...])


def _pad_lanes(v):
    return jnp.pad(v.reshape(1, -1), ((0, 0), (0, LANES - v.shape[-1])))


def _ssd_branch(proj3, conv_w, conv_b, dt_bias, a_log, d_skip, g_norm, w_b, col_xbc, col_z, col_dt):
    b, l, _ = proj3.shape
    nh = dt_bias.shape[0]
    inner = g_norm.shape[0]
    hd = inner // nh
    cdim = conv_w.shape[1]
    dm = w_b.shape[1]
    q = M2_CHUNK
    c2 = lambda i, c: (0, 0)
    kern = functools.partial(_ssd_kernel, nh=nh)
    return pl.pallas_call(
        kern,
        grid=(b, l // q),
        in_specs=[pl.BlockSpec((1, q, cdim), lambda i, c: (i, c, col_xbc)),
                  pl.BlockSpec((1, q, inner), lambda i, c: (i, c, col_z)),
                  pl.BlockSpec((1, q, LANES), lambda i, c: (i, c, col_dt)),
                  pl.BlockSpec((M2_CONV, cdim), c2), pl.BlockSpec((1, cdim), c2),
                  pl.BlockSpec((1, LANES), c2), pl.BlockSpec((1, LANES), c2),
                  pl.BlockSpec((1, inner), c2), pl.BlockSpec((1, inner), c2),
                  pl.BlockSpec((inner, dm), c2)],
        out_specs=[pl.BlockSpec((1, q, dm), lambda i, c: (i, c, 0)),
                   pl.BlockSpec((1, nh, hd, M2_STATE), lambda i, c: (i, 0, 0, 0)),
                   pl.BlockSpec((1, M2_CONV - 1, cdim), lambda i, c: (i, 0, 0))],
        out_shape=[jax.ShapeDtypeStruct((b, l, dm), F32),
                   jax.ShapeDtypeStruct((b, nh, hd, M2_STATE), F32),
                   jax.ShapeDtypeStruct((b, M2_CONV - 1, cdim), F32)],
        scratch_shapes=[pltpu.VMEM((8 + q, cdim), F32), pltpu.VMEM((nh, hd, M2_STATE), F32)],
        compiler_params=_params("arbitrary", "arbitrary"),
        name="ssd_branch",
    )(proj3, proj3, proj3, conv_w, conv_b.reshape(1, cdim), _pad_lanes(dt_bias), _pad_lanes(a_log),
      jnp.repeat(d_skip, hd).reshape(1, inner), g_norm.reshape(1, inner), w_b.astype(BF16))


def _ssd_step_prep_kernel(xbc_ref, buf_ref, dtr_ref, cw_ref, cb_ref, dtb_ref, alog_ref,
                          xbc_o, buf_o, xdt_o, da_o, *, nh, inner):
    cdim = xbc_ref.shape[1]
    raw = xbc_ref[...]
    acc = cb_ref[...] + buf_ref[:, 0:cdim] * cw_ref[0:1, :]
    for k in range(1, M2_CONV - 1):
        acc = acc + buf_ref[:, k * cdim:(k + 1) * cdim] * cw_ref[k:k + 1, :]
    acc = acc + raw * cw_ref[M2_CONV - 1:M2_CONV, :]
    xbc = jax.nn.silu(acc)
    xbc_o[...] = xbc
    for k in range(M2_CONV - 2):
        buf_o[:, k * cdim:(k + 1) * cdim] = buf_ref[:, (k + 1) * cdim:(k + 2) * cdim]
    buf_o[:, (M2_CONV - 2) * cdim:] = raw
    dt = _softplus(dtr_ref[...] + dtb_ref[...])
    da = jnp.exp(dt * (-jnp.exp(alog_ref[...])))
    xdt_o[...] = xbc[:, :inner] * jnp.dot(dt, _head_expand(nh, inner), precision=HIGHEST,
                                         preferred_element_type=F32)
    da_o[...] = jnp.dot(da, _head_expand(nh, nh * LANES), precision=HIGHEST, preferred_element_type=F32)


def _ssd_step_state_kernel(h_ref, xdt_ref, ex_ref, ext_ref, b_ref, c_ref, da_ref, hn_ref, y_ref, *, hd):
    h = pl.program_id(0)
    xrep = jnp.dot(xdt_ref[...], ex_ref[0], precision=HIGHEST, preferred_element_type=F32)
    hn = jnp.tile(da_ref[...], (1, hd)) * h_ref[...] + xrep * jnp.tile(b_ref[...], (1, hd))
    hn_ref[...] = hn
    contrib = _bdot(hn * jnp.tile(c_ref[...], (1, hd)), ext_ref[0])

    @pl.when(h % 2 == 0)
    def _():
        y_ref[...] = contrib

    @pl.when(h % 2 == 1)
    def _():
        y_ref[...] += contrib


def _ssd_epilogue_kernel(y_ref, xbc_ref, z_ref, dsk_ref, gn_ref, wb_ref, o_ref):
    inner = y_ref.shape[1]
    y = y_ref[...] + dsk_ref[...] * xbc_ref[:, :inner]
    o_ref[...] = _gated_norm_proj(y, z_ref[...], gn_ref[...], wb_ref[...])


def _ssd_step(proj, state, conv_buf, conv_w, conv_b, dt_bias, a_log, d_skip, g_norm, w_b,
              col_xbc, col_z, col_dt):
    nb = proj.shape[0]
    nh = dt_bias.shape[0]
    inner = g_norm.shape[0]
    hd = inner // nh
    ns = M2_STATE
    cdim = conv_w.shape[1]
    dm = w_b.shape[1]
    nbuf = M2_CONV - 1
    c1 = lambda i: (0, 0)
    prep = functools.partial(_ssd_step_prep_kernel, nh=nh, inner=inner)
    xbc, new_buf, xdt, da_full = pl.pallas_call(
        prep,
        grid=(1,),
        in_specs=[pl.BlockSpec((nb, cdim), lambda i: (0, col_xbc)),
                  pl.BlockSpec((nb, nbuf * cdim), c1),
                  pl.BlockSpec((nb, LANES), lambda i: (0, col_dt)),
                  pl.BlockSpec((M2_CONV, cdim), c1), pl.BlockSpec((1, cdim), c1),
                  pl.BlockSpec((1, LANES), c1), pl.BlockSpec((1, LANES), c1)],
        out_specs=[pl.BlockSpec((nb, cdim), c1), pl.BlockSpec((nb, nbuf * cdim), c1),
                   pl.BlockSpec((nb, inner), c1), pl.BlockSpec((nb, nh * LANES), c1)],
        out_shape=[jax.ShapeDtypeStruct((nb, cdim), F32), jax.ShapeDtypeStruct((nb, nbuf * cdim), F32),
                   jax.ShapeDtypeStruct((nb, inner), F32), jax.ShapeDtypeStruct((nb, nh * LANES), F32)],
        compiler_params=_params("arbitrary"),
        name="ssd_step_prep",
    )(proj, conv_buf.reshape(nb, nbuf * cdim), proj, conv_w, conv_b.reshape(1, cdim),
      _pad_lanes(dt_bias), _pad_lanes(a_log))

    row = jnp.arange(2 * hd)[:, None]
    col = jnp.arange(hd * ns)[None, :]
    ex = jnp.stack([(row == col // ns), (row == col // ns + hd)]).astype(F32)
    ext = jnp.swapaxes(ex, 1, 2).astype(BF16)
    gcol = inner // ns
    hpg = nh // M2_GROUPS
    kern = functools.partial(_ssd_step_state_kernel, hd=hd)
    hn, y = pl.pallas_call(
        kern,
        grid=(nh,),
        in_specs=[pl.BlockSpec((nb, hd * ns), lambda h: (0, h)),
                  pl.BlockSpec((nb, 2 * hd), lambda h: (0, h // 2)),
                  pl.BlockSpec((1, 2 * hd, hd * ns), lambda h: (h % 2, 0, 0)),
                  pl.BlockSpec((1, hd * ns, 2 * hd), lambda h: (h % 2, 0, 0)),
                  pl.BlockSpec((nb, ns), lambda h: (0, gcol + h // hpg)),
                  pl.BlockSpec((nb, ns), lambda h: (0, gcol + M2_GROUPS + h // hpg)),
                  pl.BlockSpec((nb, LANES), lambda h: (0, h))],
        out_specs=[pl.BlockSpec((nb, hd * ns), lambda h: (0, h)),
                   pl.BlockSpec((nb, 2 * hd), lambda h: (0, h // 2))],
        out_shape=[jax.ShapeDtypeStruct((nb, nh * hd * ns), F32), jax.ShapeDtypeStruct((nb, inner), F32)],
        compiler_params=_params("arbitrary"),
        name="ssd_step_state",
    )(state.reshape(nb, nh * hd * ns), xdt, ex, ext, xbc, xbc, da_full)

    b_out = pl.pallas_call(
        _ssd_epilogue_kernel,
        grid=(1,),
        in_specs=[pl.BlockSpec((nb, inner), c1), pl.BlockSpec((nb, cdim), c1),
                  pl.BlockSpec((nb, inner), lambda i: (0, col_z)),
                  pl.BlockSpec((1, inner), c1), pl.BlockSpec((1, inner), c1),
                  pl.BlockSpec((inner, dm), c1)],
        out_specs=pl.BlockSpec((nb, dm), c1),
        out_shape=jax.ShapeDtypeStruct((nb, dm), F32),
        compiler_params=_params("arbitrary"),
        name="ssd_step_epilogue",
    )(y, xbc, proj, jnp.repeat(d_skip, hd).reshape(1, inner), g_norm.reshape(1, inner), w_b.astype(BF16))
    return b_out, hn.reshape(nb, nh, hd, ns), new_buf.reshape(nb, nbuf, cdim)


def _merge_kernel(x_ref, a_ref, b_ref, ga_ref, gb_ref, gt_ref, sh_ref, sc_ref, g_ref, w_ref, x1_ref, h2t_ref):
    merged = jax.nn.sigmoid(ga_ref[...]) * a_ref[...] + jax.nn.sigmoid(gb_ref[...]) * b_ref[...]
    x1 = x_ref[...] + _rows(gt_ref) * _bdot(merged, w_ref[...])
    x1_ref[...] = x1
    h2 = _rmsnorm(x1, g_ref[...]) * (1.0 + _rows(sc_ref)) + _rows(sh_ref)
    h2t_ref[...] = h2.T.astype(BF16)


def _merge(x2d, a2d, b2d, proj, mod, g2, w_out, col_ga, col_gb, tt):
    t, d = x2d.shape
    tile_of = lambda i: i
    row = lambda i: (i, 0)
    return pl.pallas_call(
        _merge_kernel,
        grid=(t // tt,),
        in_specs=[pl.BlockSpec((tt, d), row), pl.BlockSpec((tt, d), row), pl.BlockSpec((tt, d), row),
                  pl.BlockSpec((tt, d), lambda i: (i, col_ga)), pl.BlockSpec((tt, d), lambda i: (i, col_gb)),
                  mod.spec(2, tile_of), mod.spec(3, tile_of), mod.spec(4, tile_of),
                  pl.BlockSpec((1, d), lambda i: (0, 0)), pl.BlockSpec((d, d), lambda i: (0, 0))],
        out_specs=[pl.BlockSpec((tt, d), row), pl.BlockSpec((d, tt), lambda i: (0, i))],
        out_shape=[jax.ShapeDtypeStruct((t, d), F32), jax.ShapeDtypeStruct((d, t), BF16)],
        compiler_params=_params("arbitrary"),
        name="merge",
    )(x2d, a2d, b2d, proj, proj, mod.arr, mod.arr, mod.arr, g2.reshape(1, d), w_out.astype(BF16))


def _top_pairs(n):
    return [(i, j) for i in range(1, n + 1) for j in range(1, n // i + 1)]


def _route_kernel(h2t_ref, wq_ref, keys_ref, c_ref, f_ref, s1_ref, e1_ref, sc_ref, top_ref, cand_ref):
    nk = PEER_NKEYS
    ntop = PEER_TOPK + 1
    tt = h2t_ref.shape[1]
    qt = jnp.dot(wq_ref[...], h2t_ref[...], preferred_element_type=F32)
    dh = qt.shape[0] // (2 * PEER_HEADS)
    neg = jnp.full((1, tt), -jnp.inf, F32)

    def extract(s, ref, n):
        for r in range(n):
            m = jnp.max(s, axis=0, keepdims=True)
            ref[r:r + 1, :] = m
            if r < n - 1:
                s = jnp.where(s >= m, -jnp.inf, s)

    for hs in range(2 * PEER_HEADS):
        s = jnp.dot(keys_ref[hs], qt[hs * dh:(hs + 1) * dh].astype(BF16), preferred_element_type=F32)
        sc_ref[hs] = s
        extract(s, top_ref.at[hs], ntop)

    pairs = _top_pairs(ntop)
    npad = cand_ref.shape[0]
    for h in range(PEER_HEADS):
        ta, tb = top_ref.at[2 * h], top_ref.at[2 * h + 1]
        for idx, (i, j) in enumerate(pairs):
            cand_ref[idx:idx + 1, :] = ta[i - 1:i, :] + tb[j - 1:j, :]
        for idx in range(len(pairs), npad):
            cand_ref[idx:idx + 1, :] = neg
        extract(cand_ref[...], top_ref.at[2 * PEER_HEADS], ntop)
        best = top_ref.at[2 * PEER_HEADS]
        v1 = best[0:1, :]
        z = jnp.zeros((1, tt), F32)
        for r in range(PEER_TOPK):
            z = z + jnp.exp(best[r:r + 1, :] - v1)
        thr = 0.5 * (best[PEER_TOPK - 1:PEER_TOPK, :] + best[PEER_TOPK:PEER_TOPK + 1, :])
        s0 = sc_ref[2 * h]
        s1 = sc_ref[2 * h + 1]
        c_ref[h] = thr - s0
        f_ref[h] = jnp.exp(s0 - ta[0:1, :]) / z
        s1_ref[h] = s1
        e1_ref[h] = jnp.exp(s1 - tb[0:1, :])


def _route(h2t, wq_t, keys, tt):
    d, t = h2t.shape
    nq = wq_t.shape[0]
    nhs = keys.shape[0]
    ntop = PEER_TOPK + 1
    npad = -(-len(_top_pairs(ntop)) // 8) * 8
    blk = pl.BlockSpec((PEER_HEADS, PEER_NKEYS, tt), lambda i: (0, 0, i))
    shp = jax.ShapeDtypeStruct((PEER_HEADS, PEER_NKEYS, t), F32)
    return pl.pallas_call(
        _route_kernel,
        grid=(t // tt,),
        in_specs=[pl.BlockSpec((d, tt), lambda i: (0, i)),
                  pl.BlockSpec((nq, d), lambda i: (0, 0)),
                  pl.BlockSpec(keys.shape, lambda i: (0, 0, 0))],
        out_specs=[blk, blk, blk, blk],
        out_shape=[shp, shp, shp, shp],
        scratch_shapes=[pltpu.VMEM((nhs, PEER_NKEYS, tt), F32),
                        pltpu.VMEM((nhs + 1, 24, tt), F32),
                        pltpu.VMEM((npad, tt), F32)],
        compiler_params=_params("arbitrary"),
        name="peer_route",
    )(h2t, wq_t, keys)


def _peer_kernel(h2t_ref, down_ref, upt_ref, c_ref, f_ref, s1_ref, e1_ref, x1_ref, gt_ref, gf_ref,
                 shf_ref, scf_ref, y_ref, acc_ref, *, final):
    e = pl.program_id(1)
    et = down_ref.shape[0]
    rows_per_step = et // PEER_NKEYS

    @pl.when(e == 0)
    def _():
        acc_ref[...] = jnp.zeros(acc_ref.shape, F32)

    act = _gelu(jnp.dot(down_ref[...], h2t_ref[...], preferred_element_type=F32))
    parts = []
    for r in range(rows_per_step):
        i = e * rows_per_step + r
        w = None
        for h in range(PEER_HEADS):
            sel = jnp.where(s1_ref[h] >= c_ref[h, pl.ds(i, 1), :], e1_ref[h], 0.0) * f_ref[h, pl.ds(i, 1), :]
            w = sel if w is None else w + sel
        parts.append((act[r * PEER_NKEYS:(r + 1) * PEER_NKEYS] * w).astype(BF16))
    acc_ref[...] += jnp.dot(upt_ref[...], jnp.concatenate(parts, axis=0), preferred_element_type=F32)

    @pl.when(e == pl.num_programs(1) - 1)
    def _():
        x2 = x1_ref[...] + _rows(gt_ref) * acc_ref[...].T
        if final:
            x2 = _rmsnorm(x2, gf_ref[...]) * (1.0 + _rows(scf_ref)) + _rows(shf_ref)
        y_ref[...] = x2


def _peer(h2t, down, up_t, route, x1, mod, mod_f, g_final, tt, et, final):
    d, t = h2t.shape
    ne = down.shape[0]
    cth, fth, s1t, e1t = route
    tile_of = lambda i, e: i
    rblk = pl.BlockSpec((PEER_HEADS, PEER_NKEYS, tt), lambda i, e: (0, 0, i))
    return pl.pallas_call(
        functools.partial(_peer_kernel, final=final),
        grid=(t // tt, ne // et),
        in_specs=[pl.BlockSpec((d, tt), lambda i, e: (0, i)),
                  pl.BlockSpec((et, d), lambda i, e: (e, 0)),
                  pl.BlockSpec((d, et), lambda i, e: (0, e)),
                  rblk, rblk, rblk, rblk,
                  pl.BlockSpec((tt, d), lambda i, e: (i, 0)),
                  mod.spec(5, tile_of),
                  pl.BlockSpec((1, d), lambda i, e: (0, 0)),
                  mod_f.spec(0, tile_of), mod_f.spec(1, tile_of)],
        out_specs=pl.BlockSpec((tt, d), lambda i, e: (i, 0)),
        out_shape=jax.ShapeDtypeStruct((t, d), F32),
        scratch_shapes=[pltpu.VMEM((d, tt), F32)],
        compiler_params=_params("arbitrary", "arbitrary"),
        name="peer_experts",
    )(h2t, down, up_t, cth, fth, s1t, e1t, x1, mod.arr, g_final.reshape(1, d), mod_f.arr, mod_f.arr)


def _reorder_w_in(w_in, s5w, inner, cdim, nh):
    o1 = s5w
    o2 = o1 + inner
    o3 = o2 + cdim
    o4 = o3 + nh
    d = w_in.shape[0]
    pad = jnp.zeros((d, LANES - nh), w_in.dtype)
    return jnp.concatenate([w_in[:, o2:o3], w_in[:, o4:], w_in[:, o1:o2], w_in[:, :o1], w_in[:, o3:o4], pad],
                           axis=1).astype(BF16)


def kernel(x_prompt, x_sample, state_s5_re, state_s5_im, state_ssm, state_conv, c_prompt, c_sample, w_ada, b_ada, g_norm1, g_norm2, w_in, s5_a_re, s5_a_im, s5_log_dt, s5_b_re, s5_b_im, s5_c_re, s5_c_im, s5_d, s5_w_glu, s5_b_glu, w_branch_a, conv_w, conv_b, dt_bias, a_log, d_skip, g_ssm_norm, w_branch_b, w_out, w_query, sub_keys, expert_down, expert_up, g_final, w_ada_final, b_ada_final):
    depth = w_ada.shape[0]
    bp, lp, d = x_prompt.shape
    bs = x_sample.shape[0]
    assert x_sample.shape[1] == 1 and lp % M2_CHUNK == 0
    g5, p5 = s5_a_re.shape[1:]
    s5w = g5 * S5_GROUP
    inner = g_ssm_norm.shape[1]
    nh = dt_bias.shape[1]
    cdim = conv_w.shape[2]
    tp = bp * lp

    col_xbc, col_ga, col_gb = 0, cdim // d, cdim // d + 1
    off_z = cdim + 2 * d
    col_z = off_z // inner
    col_u = (off_z + inner) // s5w
    col_dt = (off_z + inner + s5w) // LANES

    c_all = jnp.concatenate([c_prompt, c_sample], axis=0)
    mod_fin = _cond_proj(c_all, w_ada_final, b_ada_final)
    tt_p = 512
    modf_p = _Mod(mod_fin[:bp], d, False, lp // tt_p)
    modf_s = _Mod(mod_fin[bp:], d, True, 1)

    xp = x_prompt.reshape(tp, d)
    xs = x_sample.reshape(bs, d)
    outs_p, outs_s = [], []
    for l in range(depth):
        mod_all = _cond_proj(c_all, w_ada[l], b_ada[l])
        mod_p = _Mod(mod_all[:bp], d, False, lp // tt_p)
        mod_s = _Mod(mod_all[bp:], d, True, 1)
        final = l == depth - 1
        w_in_r = _reorder_w_in(w_in[l], s5w, inner, cdim, nh)
        s5p = _s5_prep(s5_a_re[l], s5_a_im[l], s5_log_dt[l], s5_b_re[l], s5_b_im[l], s5_c_re[l], s5_c_im[l])
        d5 = s5_d[l].reshape(1, s5w)
        wq_t = w_query[l].T.astype(BF16)
        keys = sub_keys[l].reshape(2 * PEER_HEADS, PEER_NKEYS, -1).astype(BF16)
        down = expert_down[l].astype(BF16)
        up_t = expert_up[l].T.astype(BF16)

        proj = _inproj(xp, g_norm1[l], mod_p, w_in_r, tt_p)
        proj3 = proj.reshape(bp, lp, -1)
        zeros5 = jnp.zeros((bp, g5 * p5), F32)
        a_out, sr, si = _s5_branch(proj3, col_u, zeros5, zeros5, s5p, d5, s5_w_glu[l], s5_b_glu[l],
                                   w_branch_a[l], bp, 128)
        b_out, hl, nbuf = _ssd_branch(proj3, conv_w[l], conv_b[l], dt_bias[l], a_log[l], d_skip[l],
                                      g_ssm_norm[l], w_branch_b[l], col_xbc, col_z, col_dt)
        x1, h2t = _merge(xp, a_out.reshape(tp, d), b_out.reshape(tp, d), proj, mod_p, g_norm2[l], w_out[l],
                         col_ga, col_gb, tt_p)
        route = _route(h2t, wq_t, keys, 256)
        xp = _peer(h2t, down, up_t, route, x1, mod_p, modf_p, g_final, tt_p, 512, final)
        outs_p.append((sr.reshape(bp, g5, p5), si.reshape(bp, g5, p5), hl, nbuf))

        proj_s = _inproj(xs, g_norm1[l], mod_s, w_in_r, bs)
        a_s, sr_s, si_s = _s5_branch(proj_s.reshape(1, bs, -1), col_u, state_s5_re[l].reshape(bs, g5 * p5),
                                     state_s5_im[l].reshape(bs, g5 * p5), s5p, d5, s5_w_glu[l], s5_b_glu[l],
                                     w_branch_a[l], bs, 1)
        b_s, hl_s, nbuf_s = _ssd_step(proj_s, state_ssm[l], state_conv[l], conv_w[l], conv_b[l], dt_bias[l],
                                      a_log[l], d_skip[l], g_ssm_norm[l], w_branch_b[l], col_xbc, col_z, col_dt)
        x1_s, h2t_s = _merge(xs, a_s.reshape(bs, d), b_s, proj_s, mod_s, g_norm2[l], w_out[l], col_ga, col_gb, bs)
        route_s = _route(h2t_s, wq_t, keys, bs)
        xs = _peer(h2t_s, down, up_t, route_s, x1_s, mod_s, modf_s, g_final, bs, 512, final)
        outs_s.append((sr_s.reshape(bs, g5, p5), si_s.reshape(bs, g5, p5), hl_s, nbuf_s))

    stack = lambda outs, k: jnp.stack([o[k] for o in outs])
    return (xp.reshape(bp, lp, d), xs.reshape(bs, 1, d),
            stack(outs_p, 0), stack(outs_p, 1), stack(outs_p, 2), stack(outs_p, 3),
            stack(outs_s, 0), stack(outs_s, 1), stack(outs_s, 2), stack(outs_s, 3))
```

```python
import functools
import math

import jax
import jax.numpy as jnp
from jax import lax
from jax.experimental import pallas as pl
from jax.experimental.pallas import tpu as pltpu

F32 = jnp.float32
BF16 = jnp.bfloat16
HIGHEST = lax.Precision.HIGHEST

LANES = 128
NORM_EPS = 1e-6
GNORM_EPS = 1e-5

S5_GROUP = 16
S5_STATE = 64
M2_HEADDIM = 64
M2_GROUPS = 4
M2_STATE = 128
M2_CONV = 4
M2_CHUNK = 128
PEER_HEADS = 8
PEER_NKEYS = 128
PEER_TOPK = 16

VMEM_LIMIT = 56 * 1024 * 1024


def _params(*sem):
    return pltpu.CompilerParams(dimension_semantics=sem, vmem_limit_bytes=VMEM_LIMIT)


def _bdot(a, b):
    return jnp.dot(a.astype(BF16), b.astype(BF16), preferred_element_type=F32)


def _rmsnorm(x, g):
    return x * lax.rsqrt(jnp.mean(x * x, axis=-1, keepdims=True) + NORM_EPS) * g


def _rows(ref):
    v = ref[...]
    return v.reshape(-1, v.shape[-1])


def _gelu(x):
    return 0.5 * x * (1.0 + lax.erf(x * (1.0 / math.sqrt(2.0))))


def _softplus(x):
    return jnp.maximum(x, 0.0) + jnp.log1p(jnp.exp(-jnp.abs(x)))


def _cond_kernel(c_ref, w_ref, b_ref, o_ref):
    sc = jax.nn.silu(c_ref[...])
    o_ref[...] = jnp.dot(sc, w_ref[...], precision=HIGHEST, preferred_element_type=F32) + b_ref[...]


def _cond_proj(c, w, b):
    r, d = c.shape
    n = w.shape[1]
    tn = 1024
    return pl.pallas_call(
        _cond_kernel,
        grid=(n // tn,),
        in_specs=[pl.BlockSpec((r, d), lambda j: (0, 0)),
                  pl.BlockSpec((d, tn), lambda j: (0, j)),
                  pl.BlockSpec((1, tn), lambda j: (0, j))],
        out_specs=pl.BlockSpec((r, tn), lambda j: (0, j)),
        out_shape=jax.ShapeDtypeStruct((r, n), F32),
        compiler_params=_params("arbitrary"),
        name="cond_proj",
    )(c, w, b.reshape(1, n))


class _Mod:
    def __init__(self, mod, d, per_token, tiles_per_batch):
        self.per_token = per_token
        self.tpb = tiles_per_batch
        self.d = d
        self.arr = mod if per_token else mod.reshape(mod.shape[0], 1, mod.shape[1])

    def spec(self, k, tile_of):
        if self.per_token:
            return pl.BlockSpec((self.arr.shape[0], self.d), lambda *g: (0, k))
        return pl.BlockSpec((1, 1, self.d), lambda *g: (tile_of(*g) // self.tpb, 0, k))


def _inproj_kernel(x_ref, g_ref, sh_ref, sc_ref, w_ref, o_ref):
    h = _rmsnorm(x_ref[...], g_ref[...]) * (1.0 + _rows(sc_ref)) + _rows(sh_ref)
    o_ref[...] = _bdot(h, w_ref[...])


def _inproj(x2d, g, mod, w_bf16, tt):
    t, d = x2d.shape
    n = w_bf16.shape[1]
    tn = n // 3
    tile_of = lambda j, i: i
    return pl.pallas_call(
        _inproj_kernel,
        grid=(n // tn, t // tt),
        in_specs=[pl.BlockSpec((tt, d), lambda j, i: (i, 0)),
                  pl.BlockSpec((1, d), lambda j, i: (0, 0)),
                  mod.spec(0, tile_of), mod.spec(1, tile_of),
                  pl.BlockSpec((d, tn), lambda j, i: (0, j))],
        out_specs=pl.BlockSpec((tt, tn), lambda j, i: (i, j)),
        out_shape=jax.ShapeDtypeStruct((t, n), F32),
        compiler_params=_params("arbitrary", "arbitrary"),
        name="inproj",
    )(x2d, g.reshape(1, d), mod.arr, mod.arr, w_bf16)


def _s5_kernel(u_ref, s0r_ref, s0i_ref, ar_ref, ai_ref, bre_ref, bim_ref, cre_ref, cim_ref, d_ref,
               wglu_ref, bglu_ref, wa_ref, o_ref, sr_ref, si_ref, s_ref, carry_ref, *, nb, lc):
    nt = ar_ref.shape[1] // LANES
    c = pl.program_id(0)
    u = _rows(u_ref)
    ub = u.astype(BF16)

    @pl.when(c == 0)
    def _():
        for j in range(nt):
            carry_ref[j] = s0r_ref[:, j * LANES:(j + 1) * LANES]
            carry_ref[nt + j] = s0i_ref[:, j * LANES:(j + 1) * LANES]

    gpt = LANES // S5_GROUP
    spt = LANES // S5_STATE
    for j in range(nt):
        k = (j * spt) // gpt
        uk = ub[:, k * LANES:(k + 1) * LANES]
        s_ref[j] = jnp.dot(uk, bre_ref[j], preferred_element_type=F32)
        s_ref[nt + j] = jnp.dot(uk, bim_ref[j], preferred_element_type=F32)

    def step(t, carry):
        new = []
        for j in range(nt):
            cr, ci = carry[2 * j], carry[2 * j + 1]
            ar = ar_ref[:, j * LANES:(j + 1) * LANES]
            ai = ai_ref[:, j * LANES:(j + 1) * LANES]
            idx = pl.ds(t, nb, stride=lc)
            nr = ar * cr - ai * ci + s_ref[j, idx, :]
            ni = ar * ci + ai * cr + s_ref[nt + j, idx, :]
            s_ref[j, idx, :] = nr
            s_ref[nt + j, idx, :] = ni
            new += [nr, ni]
        return tuple(new)

    init = []
    for j in range(nt):
        init += [carry_ref[j], carry_ref[nt + j]]
    fin = lax.fori_loop(0, lc, step, tuple(init))
    for j in range(nt):
        carry_ref[j] = fin[2 * j]
        carry_ref[nt + j] = fin[2 * j + 1]
        sr_ref[:, j * LANES:(j + 1) * LANES] = fin[2 * j]
        si_ref[:, j * LANES:(j + 1) * LANES] = fin[2 * j + 1]

    tpo = gpt // spt
    ys = []
    for m in range(u.shape[1] // LANES):
        acc = d_ref[:, m * LANES:(m + 1) * LANES] * u[:, m * LANES:(m + 1) * LANES]
        for k in range(tpo):
            j = tpo * m + k
            acc = acc + jnp.dot(s_ref[j].astype(BF16), cre_ref[j], preferred_element_type=F32)
            acc = acc - jnp.dot(s_ref[nt + j].astype(BF16), cim_ref[j], preferred_element_type=F32)
        ys.append(acc)
    y = _gelu(jnp.concatenate(ys, axis=1))
    y = y * jax.nn.sigmoid(_bdot(y, wglu_ref[...]) + bglu_ref[...])
    o_ref[...] = _bdot(y, wa_ref[...]).reshape(o_ref.shape)


def _s5_prep(a_re, a_im, log_dt, b_re, b_im, c_re, c_im):
    g, p = a_re.shape
    h = b_re.shape[-1]
    dt = jnp.exp(log_dt)[:, None]
    mag = jnp.exp(dt * a_re)
    abar_re = mag * jnp.cos(dt * a_im)
    abar_im = mag * jnp.sin(dt * a_im)
    den = a_re * a_re + a_im * a_im
    nr = abar_re - 1.0
    f_re = (nr * a_re + abar_im * a_im) / den
    f_im = (abar_im * a_re - nr * a_im) / den
    bbar_re = f_re[..., None] * b_re - f_im[..., None] * b_im
    bbar_im = f_re[..., None] * b_im + f_im[..., None] * b_re
    eye = jnp.eye(g, dtype=F32)
    gpt = LANES // h
    spt = LANES // p
    nt = g // spt

    def in_blocks(bbar):
        full = jnp.einsum('gph,gk->ghkp', bbar, eye).reshape(g * h, g * p)
        return jnp.stack([full[((j * spt) // gpt) * LANES:((j * spt) // gpt + 1) * LANES,
                               j * LANES:(j + 1) * LANES] for j in range(nt)]).astype(BF16)

    def out_blocks(cm):
        full = jnp.einsum('ghp,gk->gpkh', cm, eye).reshape(g * p, g * h)
        tpo = gpt // spt
        return jnp.stack([full[j * LANES:(j + 1) * LANES, (j // tpo) * LANES:(j // tpo + 1) * LANES]
                          for j in range(nt)]).astype(BF16)

    return (abar_re.reshape(1, g * p), abar_im.reshape(1, g * p), in_blocks(bbar_re), in_blocks(bbar_im),
            out_blocks(c_re), out_blocks(c_im))


def _s5_branch(proj3, col_blk, s0r, s0i, prep, d_flat, w_glu, b_glu, w_a, nb, lc):
    abar_re, abar_im, bre, bim, cre, cim = prep
    b3, l3, _ = proj3.shape
    width = w_glu.shape[0]
    dm = w_a.shape[1]
    ns = abar_re.shape[1]
    nt = ns // LANES
    blk_b = b3
    blk_l = nb * lc // b3
    nchunk = l3 // blk_l
    const2 = lambda c: (0, 0)
    const3 = lambda c: (0, 0, 0)
    kern = functools.partial(_s5_kernel, nb=nb, lc=lc)
    return pl.pallas_call(
        kern,
        grid=(nchunk,),
        in_specs=[pl.BlockSpec((blk_b, blk_l, width), lambda c: (0, c, col_blk)),
                  pl.BlockSpec((nb, ns), const2), pl.BlockSpec((nb, ns), const2),
                  pl.BlockSpec((1, ns), const2), pl.BlockSpec((1, ns), const2),
                  pl.BlockSpec((nt, LANES, LANES), const3), pl.BlockSpec((nt, LANES, LANES), const3),
                  pl.BlockSpec((nt, LANES, LANES), const3), pl.BlockSpec((nt, LANES, LANES), const3),
                  pl.BlockSpec((1, width), const2),
                  pl.BlockSpec((width, width), const2), pl.BlockSpec((1, width), const2),
                  pl.BlockSpec((width, dm), const2)],
        out_specs=[pl.BlockSpec((blk_b, blk_l, dm), lambda c: (0, c, 0)),
                   pl.BlockSpec((nb, ns), const2), pl.BlockSpec((nb, ns), const2)],
        out_shape=[jax.ShapeDtypeStruct((b3, l3, dm), F32),
                   jax.ShapeDtypeStruct((nb, ns), F32), jax.ShapeDtypeStruct((nb, ns), F32)],
        scratch_shapes=[pltpu.VMEM((2 * nt, nb * lc, LANES), F32),
                        pltpu.VMEM((2 * nt, nb, LANES), F32)],
        compiler_params=_params("arbitrary"),
        name="s5_branch",
    )(proj3, s0r, s0i, abar_re, abar_im, bre, bim, cre, cim, d_flat,
      w_glu.astype(BF16), b_glu.reshape(1, width), w_a.astype(BF16))


def _head_expand(nh, width):
    r = lax.broadcasted_iota(jnp.int32, (LANES, width), 0)
    c = lax.broadcasted_iota(jnp.int32, (LANES, width), 1)
    return jnp.where((c // (width // nh) == r) & (r < nh), 1.0, 0.0).astype(F32)


def _gated_norm_proj(y, z, gn, wb):
    y = y * jax.nn.silu(z)
    gw = y.shape[1] // M2_GROUPS
    parts = []
    for g in range(M2_GROUPS):
        yg = y[:, g * gw:(g + 1) * gw]
        parts.append(yg * lax.rsqrt(jnp.mean(yg * yg, axis=-1, keepdims=True) + GNORM_EPS))
    y = jnp.concatenate(parts, axis=1) * gn
    return _bdot(y, wb)


def _ssd_kernel(xbc_ref, z_ref, dtr_ref, cw_ref, cb_ref, dtb_ref, alog_ref, dsk_ref, gn_ref, wb_ref,
                o_ref, hout_ref, conv_ref, xp_ref, h_ref, *, nh):
    q = M2_CHUNK
    c = pl.program_id(1)
    inner = z_ref.shape[-1]
    hd = inner // nh
    hpg = nh // M2_GROUPS
    ns = M2_STATE

    @pl.when(c == 0)
    def _():
        xp_ref[0:8] = jnp.zeros((8, xp_ref.shape[1]), F32)
        h_ref[...] = jnp.zeros(h_ref.shape, F32)

    xp_ref[8:8 + q] = xbc_ref[0]
    acc = cb_ref[...] + xp_ref[5:5 + q] * cw_ref[0:1, :]
    for k in range(1, M2_CONV):
        acc = acc + xp_ref[5 + k:5 + k + q] * cw_ref[k:k + 1, :]
    xbc = jax.nn.silu(acc)
    tail = xp_ref[5 + q:8 + q]
    conv_ref[0] = tail
    xp_ref[5:8] = tail

    xs = xbc[:, :inner]
    bm = xbc[:, inner:inner + M2_GROUPS * ns]
    cm = xbc[:, inner + M2_GROUPS * ns:]

    dt = _softplus(dtr_ref[0] + dtb_ref[...])
    da = dt * (-jnp.exp(alog_ref[...]))
    ii = lax.broadcasted_iota(jnp.int32, (q, q), 0)
    jj = lax.broadcasted_iota(jnp.int32, (q, q), 1)
    lower = ii >= jj
    cs = jnp.dot(jnp.where(lower, 1.0, 0.0).astype(F32), da, precision=HIGHEST, preferred_element_type=F32)
    cst = cs.T
    cs_last = cs[q - 1:q, :]
    expand = _head_expand(nh, inner)
    dt_full = jnp.dot(dt, expand, precision=HIGHEST, preferred_element_type=F32)
    dst_full = jnp.dot(jnp.exp(cs_last - cs), expand, precision=HIGHEST, preferred_element_type=F32)
    ecs_full = jnp.dot(jnp.exp(cs), expand, precision=HIGHEST, preferred_element_type=F32)
    xdt = xs * dt_full
    xwt = (xdt * dst_full).T
    lane = lax.broadcasted_iota(jnp.int32, (q, 2 * hd), 1)

    ys = []
    for g in range(M2_GROUPS):
        bg = bm[:, g * ns:(g + 1) * ns]
        cg = cm[:, g * ns:(g + 1) * ns].astype(BF16)
        cb = lax.dot_general(cg, bg.astype(BF16), (((1,), (1,)), ((), ())), preferred_element_type=F32)
        hprev = h_ref[g * hpg:(g + 1) * hpg].reshape(hpg * hd, ns)
        yoff = lax.dot_general(cg, hprev.astype(BF16), (((1,), (1,)), ((), ())), preferred_element_type=F32)
        yoff = yoff * ecs_full[:, g * hpg * hd:(g + 1) * hpg * hd]
        for pr in range(hpg // 2):
            h0 = g * hpg + 2 * pr
            ms = []
            for h in (h0, h0 + 1):
                seg = cs[:, h:h + 1] - cst[h:h + 1, :]
                ms.append((cb * jnp.exp(jnp.where(lower, seg, -jnp.inf))).astype(BF16))
            xp = xdt[:, h0 * hd:(h0 + 2) * hd]
            rhs = jnp.concatenate([jnp.where(lane < hd, xp, 0.0), jnp.where(lane >= hd, xp, 0.0)], axis=0)
            ydiag = jnp.dot(jnp.concatenate(ms, axis=1), rhs.astype(BF16), preferred_element_type=F32)
            ys.append(ydiag + yoff[:, 2 * pr * hd:(2 * pr + 2) * hd])
        st = jnp.dot(xwt[g * hpg * hd:(g + 1) * hpg * hd, :].astype(BF16), bg.astype(BF16),
                     preferred_element_type=F32)
        for hh in range(hpg):
            h = g * hpg + hh
            dec = jnp.exp(cst[h:h + 1, q - 1:q])
            h_ref[h] = dec * h_ref[h] + st[hh * hd:(hh + 1) * hd]
    hout_ref[0] = h_ref[...]
    y = jnp.concatenate(ys, axis=1) + dsk_ref[...] * xs
    o_ref[0] = _gated_norm_proj(y, z_ref[0], gn_ref[...], wb_ref[...])


def _pad_lanes(v):
    return jnp.pad(v.reshape(1, -1), ((0, 0), (0, LANES - v.shape[-1])))


def _ssd_branch(proj3, conv_w, conv_b, dt_bias, a_log, d_skip, g_norm, w_b, col_xbc, col_z, col_dt):
    b, l, _ = proj3.shape
    nh = dt_bias.shape[0]
    inner = g_norm.shape[0]
    hd = inner // nh
    cdim = conv_w.shape[1]
    dm = w_b.shape[1]
    q = M2_CHUNK
    c2 = lambda i, c: (0, 0)
    kern = functools.partial(_ssd_kernel, nh=nh)
    return pl.pallas_call(
        kern,
        grid=(b, l // q),
        in_specs=[pl.BlockSpec((1, q, cdim), lambda i, c: (i, c, col_xbc)),
                  pl.BlockSpec((1, q, inner), lambda i, c: (i, c, col_z)),
                  pl.BlockSpec((1, q, LANES), lambda i, c: (i, c, col_dt)),
                  pl.BlockSpec((M2_CONV, cdim), c2), pl.BlockSpec((1, cdim), c2),
                  pl.BlockSpec((1, LANES), c2), pl.BlockSpec((1, LANES), c2),
                  pl.BlockSpec((1, inner), c2), pl.BlockSpec((1, inner), c2),
                  pl.BlockSpec((inner, dm), c2)],
        out_specs=[pl.BlockSpec((1, q, dm), lambda i, c: (i, c, 0)),
                   pl.BlockSpec((1, nh, hd, M2_STATE), lambda i, c: (i, 0, 0, 0)),
                   pl.BlockSpec((1, M2_CONV - 1, cdim), lambda i, c: (i, 0, 0))],
        out_shape=[jax.ShapeDtypeStruct((b, l, dm), F32),
                   jax.ShapeDtypeStruct((b, nh, hd, M2_STATE), F32),
                   jax.ShapeDtypeStruct((b, M2_CONV - 1, cdim), F32)],
        scratch_shapes=[pltpu.VMEM((8 + q, cdim), F32), pltpu.VMEM((nh, hd, M2_STATE), F32)],
        compiler_params=_params("arbitrary", "arbitrary"),
        name="ssd_branch",
    )(proj3, proj3, proj3, conv_w, conv_b.reshape(1, cdim), _pad_lanes(dt_bias), _pad_lanes(a_log),
      jnp.repeat(d_skip, hd).reshape(1, inner), g_norm.reshape(1, inner), w_b.astype(BF16))


def _ssd_step_prep_kernel(xbc_ref, buf_ref, dtr_ref, cw_ref, cb_ref, dtb_ref, alog_ref,
                          xbc_o, buf_o, xdt_o, da_o, *, nh, inner):
    cdim = xbc_ref.shape[1]
    raw = xbc_ref[...]
    acc = cb_ref[...] + buf_ref[:, 0:cdim] * cw_ref[0:1, :]
    for k in range(1, M2_CONV - 1):
        acc = acc + buf_ref[:, k * cdim:(k + 1) * cdim] * cw_ref[k:k + 1, :]
    acc = acc + raw * cw_ref[M2_CONV - 1:M2_CONV, :]
    xbc = jax.nn.silu(acc)
    xbc_o[...] = xbc
    for k in range(M2_CONV - 2):
        buf_o[:, k * cdim:(k + 1) * cdim] = buf_ref[:, (k + 1) * cdim:(k + 2) * cdim]
    buf_o[:, (M2_CONV - 2) * cdim:] = raw
    dt = _softplus(dtr_ref[...] + dtb_ref[...])
    da = jnp.exp(dt * (-jnp.exp(alog_ref[...])))
    xdt_o[...] = xbc[:, :inner] * jnp.dot(dt, _head_expand(nh, inner), precision=HIGHEST,
                                         preferred_element_type=F32)
    da_o[...] = jnp.dot(da, _head_expand(nh, nh * LANES), precision=HIGHEST, preferred_element_type=F32)


def _ssd_step_state_kernel(h_ref, xdt_ref, ex_ref, ext_ref, b_ref, c_ref, da_ref, hn_ref, y_ref, *, hd):
    h = pl.program_id(0)
    xrep = jnp.dot(xdt_ref[...], ex_ref[0], precision=HIGHEST, preferred_element_type=F32)
    hn = jnp.tile(da_ref[...], (1, hd)) * h_ref[...] + xrep * jnp.tile(b_ref[...], (1, hd))
    hn_ref[...] = hn
    contrib = _bdot(hn * jnp.tile(c_ref[...], (1, hd)), ext_ref[0])

    @pl.when(h % 2 == 0)
    def _():
        y_ref[...] = contrib

    @pl.when(h % 2 == 1)
    def _():
        y_ref[...] += contrib


def _ssd_epilogue_kernel(y_ref, xbc_ref, z_ref, dsk_ref, gn_ref, wb_ref, o_ref):
    inner = y_ref.shape[1]
    y = y_ref[...] + dsk_ref[...] * xbc_ref[:, :inner]
    o_ref[...] = _gated_norm_proj(y, z_ref[...], gn_ref[...], wb_ref[...])


def _ssd_step(proj, state, conv_buf, conv_w, conv_b, dt_bias, a_log, d_skip, g_norm, w_b,
              col_xbc, col_z, col_dt):
    nb = proj.shape[0]
    nh = dt_bias.shape[0]
    inner = g_norm.shape[0]
    hd = inner // nh
    ns = M2_STATE
    cdim = conv_w.shape[1]
    dm = w_b.shape[1]
    nbuf = M2_CONV - 1
    c1 = lambda i: (0, 0)
    prep = functools.partial(_ssd_step_prep_kernel, nh=nh, inner=inner)
    xbc, new_buf, xdt, da_full = pl.pallas_call(
        prep,
        grid=(1,),
        in_specs=[pl.BlockSpec((nb, cdim), lambda i: (0, col_xbc)),
                  pl.BlockSpec((nb, nbuf * cdim), c1),
                  pl.BlockSpec((nb, LANES), lambda i: (0, col_dt)),
                  pl.BlockSpec((M2_CONV, cdim), c1), pl.BlockSpec((1, cdim), c1),
                  pl.BlockSpec((1, LANES), c1), pl.BlockSpec((1, LANES), c1)],
        out_specs=[pl.BlockSpec((nb, cdim), c1), pl.BlockSpec((nb, nbuf * cdim), c1),
                   pl.BlockSpec((nb, inner), c1), pl.BlockSpec((nb, nh * LANES), c1)],
        out_shape=[jax.ShapeDtypeStruct((nb, cdim), F32), jax.ShapeDtypeStruct((nb, nbuf * cdim), F32),
                   jax.ShapeDtypeStruct((nb, inner), F32), jax.ShapeDtypeStruct((nb, nh * LANES), F32)],
        compiler_params=_params("arbitrary"),
        name="ssd_step_prep",
    )(proj, conv_buf.reshape(nb, nbuf * cdim), proj, conv_w, conv_b.reshape(1, cdim),
      _pad_lanes(dt_bias), _pad_lanes(a_log))

    row = jnp.arange(2 * hd)[:, None]
    col = jnp.arange(hd * ns)[None, :]
    ex = jnp.stack([(row == col // ns), (row == col // ns + hd)]).astype(F32)
    ext = jnp.swapaxes(ex, 1, 2).astype(BF16)
    gcol = inner // ns
    hpg = nh // M2_GROUPS
    kern = functools.partial(_ssd_step_state_kernel, hd=hd)
    hn, y = pl.pallas_call(
        kern,
        grid=(nh,),
        in_specs=[pl.BlockSpec((nb, hd * ns), lambda h: (0, h)),
                  pl.BlockSpec((nb, 2 * hd), lambda h: (0, h // 2)),
                  pl.BlockSpec((1, 2 * hd, hd * ns), lambda h: (h % 2, 0, 0)),
                  pl.BlockSpec((1, hd * ns, 2 * hd), lambda h: (h % 2, 0, 0)),
                  pl.BlockSpec((nb, ns), lambda h: (0, gcol + h // hpg)),
                  pl.BlockSpec((nb, ns), lambda h: (0, gcol + M2_GROUPS + h // hpg)),
                  pl.BlockSpec((nb, LANES), lambda h: (0, h))],
        out_specs=[pl.BlockSpec((nb, hd * ns), lambda h: (0, h)),
                   pl.BlockSpec((nb, 2 * hd), lambda h: (0, h // 2))],
        out_shape=[jax.ShapeDtypeStruct((nb, nh * hd * ns), F32), jax.ShapeDtypeStruct((nb, inner), F32)],
        compiler_params=_params("arbitrary"),
        name="ssd_step_state",
    )(state.reshape(nb, nh * hd * ns), xdt, ex, ext, xbc, xbc, da_full)

    b_out = pl.pallas_call(
        _ssd_epilogue_kernel,
        grid=(1,),
        in_specs=[pl.BlockSpec((nb, inner), c1), pl.BlockSpec((nb, cdim), c1),
                  pl.BlockSpec((nb, inner), lambda i: (0, col_z)),
                  pl.BlockSpec((1, inner), c1), pl.BlockSpec((1, inner), c1),
                  pl.BlockSpec((inner, dm), c1)],
        out_specs=pl.BlockSpec((nb, dm), c1),
        out_shape=jax.ShapeDtypeStruct((nb, dm), F32),
        compiler_params=_params("arbitrary"),
        name="ssd_step_epilogue",
    )(y, xbc, proj, jnp.repeat(d_skip, hd).reshape(1, inner), g_norm.reshape(1, inner), w_b.astype(BF16))
    return b_out, hn.reshape(nb, nh, hd, ns), new_buf.reshape(nb, nbuf, cdim)


def _merge_kernel(x_ref, a_ref, b_ref, ga_ref, gb_ref, gt_ref, sh_ref, sc_ref, g_ref, w_ref, x1_ref, h2t_ref):
    merged = jax.nn.sigmoid(ga_ref[...]) * a_ref[...] + jax.nn.sigmoid(gb_ref[...]) * b_ref[...]
    x1 = x_ref[...] + _rows(gt_ref) * _bdot(merged, w_ref[...])
    x1_ref[...] = x1
    h2 = _rmsnorm(x1, g_ref[...]) * (1.0 + _rows(sc_ref)) + _rows(sh_ref)
    h2t_ref[...] = h2.T.astype(BF16)


def _merge(x2d, a2d, b2d, proj, mod, g2, w_out, col_ga, col_gb, tt):
    t, d = x2d.shape
    tile_of = lambda i: i
    row = lambda i: (i, 0)
    return pl.pallas_call(
        _merge_kernel,
        grid=(t // tt,),
        in_specs=[pl.BlockSpec((tt, d), row), pl.BlockSpec((tt, d), row), pl.BlockSpec((tt, d), row),
                  pl.BlockSpec((tt, d), lambda i: (i, col_ga)), pl.BlockSpec((tt, d), lambda i: (i, col_gb)),
                  mod.spec(2, tile_of), mod.spec(3, tile_of), mod.spec(4, tile_of),
                  pl.BlockSpec((1, d), lambda i: (0, 0)), pl.BlockSpec((d, d), lambda i: (0, 0))],
        out_specs=[pl.BlockSpec((tt, d), row), pl.BlockSpec((d, tt), lambda i: (0, i))],
        out_shape=[jax.ShapeDtypeStruct((t, d), F32), jax.ShapeDtypeStruct((d, t), BF16)],
        compiler_params=_params("arbitrary"),
        name="merge",
    )(x2d, a2d, b2d, proj, proj, mod.arr, mod.arr, mod.arr, g2.reshape(1, d), w_out.astype(BF16))


def _top_pairs(n):
    return [(i, j) for i in range(1, n + 1) for j in range(1, n // i + 1)]


def _rot_tile(lt, k, tt):
    p = (lt + k) % (tt // LANES)
    return slice(p * LANES, (p + 1) * LANES)


def _route_kernel(h2t_ref, wq_ref, keys_ref, c_ref, f_ref, s1_ref, e1_ref, sc_ref, top_ref, cand_ref):
    nk = PEER_NKEYS
    ntop = PEER_TOPK + 1
    tt = h2t_ref.shape[1]
    qt = jnp.dot(wq_ref[...], h2t_ref[...], preferred_element_type=F32)
    dh = qt.shape[0] // (2 * PEER_HEADS)
    neg = jnp.full((1, tt), -jnp.inf, F32)

    def extract(s, ref, n):
        for r in range(n):
            m = jnp.max(s, axis=0, keepdims=True)
            ref[r:r + 1, :] = m
            if r < n - 1:
                s = jnp.where(s >= m, -jnp.inf, s)

    for hs in range(2 * PEER_HEADS):
        s = jnp.dot(keys_ref[hs], qt[hs * dh:(hs + 1) * dh].astype(BF16), preferred_element_type=F32)
        sc_ref[hs] = s
        extract(s, top_ref.at[hs], ntop)

    pairs = _top_pairs(ntop)
    npad = cand_ref.shape[0]
    for h in range(PEER_HEADS):
        ta, tb = top_ref.at[2 * h], top_ref.at[2 * h + 1]
        for idx, (i, j) in enumerate(pairs):
            cand_ref[idx:idx + 1, :] = ta[i - 1:i, :] + tb[j - 1:j, :]
        for idx in range(len(pairs), npad):
            cand_ref[idx:idx + 1, :] = neg
        extract(cand_ref[...], top_ref.at[2 * PEER_HEADS], ntop)
        best = top_ref.at[2 * PEER_HEADS]
        v1 = best[0:1, :]
        z = jnp.zeros((1, tt), F32)
        for r in range(PEER_TOPK):
            z = z + jnp.exp(best[r:r + 1, :] - v1)
        thr = 0.5 * (best[PEER_TOPK - 1:PEER_TOPK, :] + best[PEER_TOPK:PEER_TOPK + 1, :])
        s0 = sc_ref[2 * h]
        s1 = sc_ref[2 * h + 1]
        c_ref[h] = thr - s0
        f_ref[h] = jnp.exp(s0 - ta[0:1, :]) * (0.5 / z)
        e1 = jnp.exp(s1 - tb[0:1, :])
        for lt in range(tt // LANES):
            src = slice(lt * LANES, (lt + 1) * LANES)
            s1_ref[h, :, _rot_tile(lt, 2 * h, tt)] = s1[:, src]
            e1_ref[h, :, _rot_tile(lt, 2 * h + 1, tt)] = e1[:, src]


def _route(h2t, wq_t, keys, tt):
    d, t = h2t.shape
    nq = wq_t.shape[0]
    nhs = keys.shape[0]
    ntop = PEER_TOPK + 1
    npad = -(-len(_top_pairs(ntop)) // 8) * 8
    blk = pl.BlockSpec((PEER_HEADS, PEER_NKEYS, tt), lambda i: (0, 0, i))
    shp = jax.ShapeDtypeStruct((PEER_HEADS, PEER_NKEYS, t), F32)
    return pl.pallas_call(
        _route_kernel,
        grid=(t // tt,),
        in_specs=[pl.BlockSpec((d, tt), lambda i: (0, i)),
                  pl.BlockSpec((nq, d), lambda i: (0, 0)),
                  pl.BlockSpec(keys.shape, lambda i: (0, 0, 0))],
        out_specs=[blk, blk, blk, blk],
        out_shape=[shp, shp, shp, shp],
        scratch_shapes=[pltpu.VMEM((nhs, PEER_NKEYS, tt), F32),
                        pltpu.VMEM((nhs + 1, 24, tt), F32),
                        pltpu.VMEM((npad, tt), F32)],
        compiler_params=_params("arbitrary"),
        name="peer_route",
    )(h2t, wq_t, keys)


def _tree_sum(xs):
    while len(xs) > 1:
        xs = [xs[k] + xs[k + 1] for k in range(0, len(xs) - 1, 2)] + ([xs[-1]] if len(xs) % 2 else [])
    return xs[0]


def _peer_kernel(h2t_ref, down_ref, upt_ref, c_ref, f_ref, s1_ref, e1_ref, x1_ref, gt_ref, gf_ref,
                 shf_ref, scf_ref, y_ref, acc_ref, d0_ref, d1_ref, a0_ref, a1_ref, *, final, ne, nsteps):
    g = pl.program_id(0)
    et, tt = d0_ref.shape
    rps = et // PEER_NKEYS
    dm = acc_ref.shape[0]
    t1 = jnp.clip(g - 1, 0, nsteps - 1)
    t2 = g - 2

    @pl.when(g == 0)
    def _():
        d1_ref[...] = jnp.zeros(d1_ref.shape, F32)
        a0_ref[...] = jnp.zeros(a0_ref.shape, BF16)
        a1_ref[...] = jnp.zeros(a1_ref.shape, BF16)

    @pl.when((g == 0) | ((t2 >= 0) & (t2 % ne == 0)))
    def _():
        acc_ref[...] = jnp.zeros(acc_ref.shape, F32)

    half = 8
    nlt = tt // LANES
    n_down, n_up = rps, 2 * rps

    def stages(d_new, d_cur, a_new, a_cur):
        def down_chunk(ck):
            rows = slice(ck * (et // n_down), (ck + 1) * (et // n_down))
            d_new[rows, :] = jnp.dot(down_ref[rows, :], h2t_ref[...], preferred_element_type=F32)

        def up_chunk(ck):
            rows = slice(ck * (dm // n_up), (ck + 1) * (dm // n_up))
            acc_ref[rows, :] += jnp.dot(upt_ref[rows, :], a_cur[...], preferred_element_type=F32)

        def weights_block(r, lt):
            i = (t1 % ne) * rps + r
            ls = slice(lt * LANES, (lt + 1) * LANES)
            cs = [jnp.broadcast_to(c_ref[h, pl.ds(i, 1), :][:, ls], (half, LANES)) for h in range(PEER_HEADS)]
            fs = [jnp.broadcast_to(f_ref[h, pl.ds(i, 1), :][:, ls], (half, LANES)) for h in range(PEER_HEADS)]
            for sb in range(PEER_NKEYS // (2 * half)):
                blk = []
                for hf in range(2):
                    j0 = sb * 2 * half + hf * half
                    js = slice(j0, j0 + half)
                    w = _tree_sum([jnp.where(s1_ref[h, js, _rot_tile(lt, 2 * h, tt)] >= cs[h],
                                             e1_ref[h, js, _rot_tile(lt, 2 * h + 1, tt)], 0.0) * fs[h]
                                   for h in range(PEER_HEADS)])
                    x = d_cur[r * PEER_NKEYS + j0:r * PEER_NKEYS + j0 + half, ls]
                    blk.append(x * (1.0 + lax.erf(x * (1.0 / math.sqrt(2.0)))) * w)
                a_new[r * PEER_NKEYS + sb * 2 * half:r * PEER_NKEYS + (sb + 1) * 2 * half, ls] = (
                    jnp.concatenate(blk, axis=0).astype(BF16))

        units = []
        k1 = k2 = 0
        while k1 < n_down or k2 < n_up:
            if k1 < n_down and k1 * n_up <= k2 * n_down:
                units.append((down_chunk, k1))
                k1 += 1
            else:
                units.append((up_chunk, k2))
                k2 += 1
        blocks = [(r, lt) for r in range(rps) for lt in range(nlt)]
        for k, (r, lt) in enumerate(blocks):
            for u, (fn, ck) in enumerate(units):
                if (u * len(blocks)) // len(units) == k:
                    fn(ck)
            weights_block(r, lt)

    @pl.when(g % 2 == 0)
    def _():
        stages(d0_ref, d1_ref, a1_ref, a0_ref)

    @pl.when(g % 2 == 1)
    def _():
        stages(d1_ref, d0_ref, a0_ref, a1_ref)

    @pl.when((t2 >= 0) & (t2 % ne == ne - 1))
    def _():
        x2 = x1_ref[...] + _rows(gt_ref) * acc_ref[...].T
        if final:
            x2 = _rmsnorm(x2, gf_ref[...]) * (1.0 + _rows(scf_ref)) + _rows(shf_ref)
        y_ref[...] = x2


def _peer(h2t, down, up_t, route, x1, mod, mod_f, g_final, tt, et, final):
    d, t = h2t.shape
    ne = down.shape[0] // et
    nt = t // tt
    nsteps = nt * ne
    cth, fth, s1t, e1t = route
    lag = lambda g, k: jnp.clip(g - k, 0, nsteps - 1)
    tok = lambda g, k: lag(g, k) // ne
    exp = lambda g, k: lag(g, k) % ne
    tile_of = lambda g: tok(g, 2)
    rblk = pl.BlockSpec((PEER_HEADS, PEER_NKEYS, tt), lambda g: (0, 0, tok(g, 1)))
    return pl.pallas_call(
        functools.partial(_peer_kernel, final=final, ne=ne, nsteps=nsteps),
        grid=(nsteps + 2,),
        in_specs=[pl.BlockSpec((d, tt), lambda g: (0, tok(g, 0))),
                  pl.BlockSpec((et, d), lambda g: (exp(g, 0), 0)),
                  pl.BlockSpec((d, et), lambda g: (0, exp(g, 2))),
                  rblk, rblk, rblk, rblk,
                  pl.BlockSpec((tt, d), lambda g: (tok(g, 2), 0)),
                  mod.spec(5, tile_of),
                  pl.BlockSpec((1, d), lambda g: (0, 0)),
                  mod_f.spec(0, tile_of), mod_f.spec(1, tile_of)],
        out_specs=pl.BlockSpec((tt, d), lambda g: (tok(g, 2), 0)),
        out_shape=jax.ShapeDtypeStruct((t, d), F32),
        scratch_shapes=[pltpu.VMEM((d, tt), F32), pltpu.VMEM((et, tt), F32), pltpu.VMEM((et, tt), F32),
                        pltpu.VMEM((et, tt), BF16), pltpu.VMEM((et, tt), BF16)],
        compiler_params=_params("arbitrary"),
        name="peer_experts",
    )(h2t, down, up_t, cth, fth, s1t, e1t, x1, mod.arr, g_final.reshape(1, d), mod_f.arr, mod_f.arr)


def _reorder_w_in(w_in, s5w, inner, cdim, nh):
    o1 = s5w
    o2 = o1 + inner
    o3 = o2 + cdim
    o4 = o3 + nh
    d = w_in.shape[0]
    pad = jnp.zeros((d, LANES - nh), w_in.dtype)
    return jnp.concatenate([w_in[:, o2:o3], w_in[:, o4:], w_in[:, o1:o2], w_in[:, :o1], w_in[:, o3:o4], pad],
                           axis=1).astype(BF16)


def kernel(x_prompt, x_sample, state_s5_re, state_s5_im, state_ssm, state_conv, c_prompt, c_sample, w_ada, b_ada, g_norm1, g_norm2, w_in, s5_a_re, s5_a_im, s5_log_dt, s5_b_re, s5_b_im, s5_c_re, s5_c_im, s5_d, s5_w_glu, s5_b_glu, w_branch_a, conv_w, conv_b, dt_bias, a_log, d_skip, g_ssm_norm, w_branch_b, w_out, w_query, sub_keys, expert_down, expert_up, g_final, w_ada_final, b_ada_final):
    depth = w_ada.shape[0]
    bp, lp, d = x_prompt.shape
    bs = x_sample.shape[0]
    assert x_sample.shape[1] == 1 and lp % M2_CHUNK == 0
    g5, p5 = s5_a_re.shape[1:]
    s5w = g5 * S5_GROUP
    inner = g_ssm_norm.shape[1]
    nh = dt_bias.shape[1]
    cdim = conv_w.shape[2]
    tp = bp * lp

    col_xbc, col_ga, col_gb = 0, cdim // d, cdim // d + 1
    off_z = cdim + 2 * d
    col_z = off_z // inner
    col_u = (off_z + inner) // s5w
    col_dt = (off_z + inner + s5w) // LANES

    c_all = jnp.concatenate([c_prompt, c_sample], axis=0)
    mod_fin = _cond_proj(c_all, w_ada_final, b_ada_final)
    tt_p = 512
    modf_p = _Mod(mod_fin[:bp], d, False, lp // tt_p)
    modf_s = _Mod(mod_fin[bp:], d, True, 1)

    xp = x_prompt.reshape(tp, d)
    xs = x_sample.reshape(bs, d)
    outs_p, outs_s = [], []
    for l in range(depth):
        mod_all = _cond_proj(c_all, w_ada[l], b_ada[l])
        mod_p = _Mod(mod_all[:bp], d, False, lp // tt_p)
        mod_s = _Mod(mod_all[bp:], d, True, 1)
        final = l == depth - 1
        w_in_r = _reorder_w_in(w_in[l], s5w, inner, cdim, nh)
        s5p = _s5_prep(s5_a_re[l], s5_a_im[l], s5_log_dt[l], s5_b_re[l], s5_b_im[l], s5_c_re[l], s5_c_im[l])
        d5 = s5_d[l].reshape(1, s5w)
        wq_t = w_query[l].T.astype(BF16)
        keys = sub_keys[l].reshape(2 * PEER_HEADS, PEER_NKEYS, -1).astype(BF16)
        down = expert_down[l].astype(BF16)
        up_t = expert_up[l].T.astype(BF16)

        proj = _inproj(xp, g_norm1[l], mod_p, w_in_r, tt_p)
        proj3 = proj.reshape(bp, lp, -1)
        zeros5 = jnp.zeros((bp, g5 * p5), F32)
        a_out, sr, si = _s5_branch(proj3, col_u, zeros5, zeros5, s5p, d5, s5_w_glu[l], s5_b_glu[l],
                                   w_branch_a[l], bp, 128)
        b_out, hl, nbuf = _ssd_branch(proj3, conv_w[l], conv_b[l], dt_bias[l], a_log[l], d_skip[l],
                                      g_ssm_norm[l], w_branch_b[l], col_xbc, col_z, col_dt)
        x1, h2t = _merge(xp, a_out.reshape(tp, d), b_out.reshape(tp, d), proj, mod_p, g_norm2[l], w_out[l],
                         col_ga, col_gb, tt_p)
        route = _route(h2t, wq_t, keys, tt_p)
        xp = _peer(h2t, down, up_t, route, x1, mod_p, modf_p, g_final, tt_p, 512, final)
        outs_p.append((sr.reshape(bp, g5, p5), si.reshape(bp, g5, p5), hl, nbuf))

        proj_s = _inproj(xs, g_norm1[l], mod_s, w_in_r, bs)
        a_s, sr_s, si_s = _s5_branch(proj_s.reshape(1, bs, -1), col_u, state_s5_re[l].reshape(bs, g5 * p5),
                                     state_s5_im[l].reshape(bs, g5 * p5), s5p, d5, s5_w_glu[l], s5_b_glu[l],
                                     w_branch_a[l], bs, 1)
        b_s, hl_s, nbuf_s = _ssd_step(proj_s, state_ssm[l], state_conv[l], conv_w[l], conv_b[l], dt_bias[l],
                                      a_log[l], d_skip[l], g_ssm_norm[l], w_branch_b[l], col_xbc, col_z, col_dt)
        x1_s, h2t_s = _merge(xs, a_s.reshape(bs, d), b_s, proj_s, mod_s, g_norm2[l], w_out[l], col_ga, col_gb, bs)
        route_s = _route(h2t_s, wq_t, keys, bs)
        xs = _peer(h2t_s, down, up_t, route_s, x1_s, mod_s, modf_s, g_final, bs, 512, final)
        outs_s.append((sr_s.reshape(bs, g5, p5), si_s.reshape(bs, g5, p5), hl_s, nbuf_s))

    stack = lambda outs, k: jnp.stack([o[k] for o in outs])
    return (xp.reshape(bp, lp, d), xs.reshape(bs, 1, d),
            stack(outs_p, 0), stack(outs_p, 1), stack(outs_p, 2), stack(outs_p, 3),
            stack(outs_s, 0), stack(outs_s, 1), stack(outs_s, 2), stack(outs_s, 3))
```

```python
import functools
import math

import jax
import jax.numpy as jnp
import numpy as np
from jax import lax
from jax.experimental import pallas as pl
from jax.experimental.pallas import tpu as pltpu

F32 = jnp.float32
BF16 = jnp.bfloat16
HIGHEST = lax.Precision.HIGHEST

LANES = 128
NORM_EPS = 1e-6
GNORM_EPS = 1e-5

S5_GROUP = 16
S5_STATE = 64
M2_HEADDIM = 64
M2_GROUPS = 4
M2_STATE = 128
M2_CONV = 4
M2_CHUNK = 128
PEER_HEADS = 8
PEER_NKEYS = 128
PEER_TOPK = 16

VMEM_LIMIT = 56 * 1024 * 1024


def _params(*sem):
    return pltpu.CompilerParams(dimension_semantics=sem, vmem_limit_bytes=VMEM_LIMIT)


def _bdot(a, b):
    return jnp.dot(a.astype(BF16), b.astype(BF16), preferred_element_type=F32)


def _rmsnorm(x, g):
    return x * lax.rsqrt(jnp.mean(x * x, axis=-1, keepdims=True) + NORM_EPS) * g


def _rows(ref):
    v = ref[...]
    return v.reshape(-1, v.shape[-1])


def _gelu(x):
    return 0.5 * x * (1.0 + lax.erf(x * (1.0 / math.sqrt(2.0))))


def _softplus(x):
    return jnp.maximum(x, 0.0) + jnp.log1p(jnp.exp(-jnp.abs(x)))


def _cond_kernel(c_ref, w_ref, b_ref, o_ref):
    sc = jax.nn.silu(c_ref[...])
    o_ref[...] = jnp.dot(sc, w_ref[...], precision=HIGHEST, preferred_element_type=F32) + b_ref[...]


def _cond_proj(c, w, b):
    r, d = c.shape
    n = w.shape[1]
    tn = 1024
    return pl.pallas_call(
        _cond_kernel,
        grid=(n // tn,),
        in_specs=[pl.BlockSpec((r, d), lambda j: (0, 0)),
                  pl.BlockSpec((d, tn), lambda j: (0, j)),
                  pl.BlockSpec((1, tn), lambda j: (0, j))],
        out_specs=pl.BlockSpec((r, tn), lambda j: (0, j)),
        out_shape=jax.ShapeDtypeStruct((r, n), F32),
        compiler_params=_params("arbitrary"),
        name="cond_proj",
    )(c, w, b.reshape(1, n))


class _Mod:
    def __init__(self, mod, d, per_token, tiles_per_batch):
        self.per_token = per_token
        self.tpb = tiles_per_batch
        self.d = d
        self.arr = mod if per_token else mod.reshape(mod.shape[0], 1, mod.shape[1])

    def spec(self, k, tile_of):
        if self.per_token:
            return pl.BlockSpec((self.arr.shape[0], self.d), lambda *g: (0, k))
        return pl.BlockSpec((1, 1, self.d), lambda *g: (tile_of(*g) // self.tpb, 0, k))


def _inproj_kernel(x_ref, g_ref, sh_ref, sc_ref, w_ref, o_ref):
    h = _rmsnorm(x_ref[...], g_ref[...]) * (1.0 + _rows(sc_ref)) + _rows(sh_ref)
    o_ref[...] = _bdot(h, w_ref[...])


def _inproj(x2d, g, mod, w_bf16, tt):
    t, d = x2d.shape
    n = w_bf16.shape[1]
    tn = n // 3
    tile_of = lambda j, i: i
    return pl.pallas_call(
        _inproj_kernel,
        grid=(n // tn, t // tt),
        in_specs=[pl.BlockSpec((tt, d), lambda j, i: (i, 0)),
                  pl.BlockSpec((1, d), lambda j, i: (0, 0)),
                  mod.spec(0, tile_of), mod.spec(1, tile_of),
                  pl.BlockSpec((d, tn), lambda j, i: (0, j))],
        out_specs=pl.BlockSpec((tt, tn), lambda j, i: (i, j)),
        out_shape=jax.ShapeDtypeStruct((t, n), F32),
        compiler_params=_params("arbitrary", "arbitrary"),
        name="inproj",
    )(x2d, g.reshape(1, d), mod.arr, mod.arr, w_bf16)


def _s5_kernel(u_ref, s0r_ref, s0i_ref, ar_ref, ai_ref, bre_ref, bim_ref, cre_ref, cim_ref, d_ref,
               wglu_ref, bglu_ref, wa_ref, o_ref, sr_ref, si_ref, s_ref, carry_ref, ubm_ref, utm_ref, *, nb, lc):
    nt = ar_ref.shape[1] // LANES
    nu = ubm_ref.shape[0]
    c = pl.program_id(0)

    u_bm = _rows(u_ref)
    for k in range(nu):
        ubm_ref[k] = u_bm[:, k * LANES:(k + 1) * LANES]

    def to_time_major(t, _):
        dst = pl.ds(pl.multiple_of(t * nb, nb), nb)
        for k in range(nu):
            utm_ref[k, dst, :] = ubm_ref[k, pl.ds(t, nb, stride=lc), :]
        return 0

    lax.fori_loop(0, lc, to_time_major, 0)
    u = jnp.concatenate([utm_ref[k] for k in range(nu)], axis=1)
    ub = u.astype(BF16)

    @pl.when(c == 0)
    def _():
        for j in range(nt):
            carry_ref[j] = s0r_ref[:, j * LANES:(j + 1) * LANES]
            carry_ref[nt + j] = s0i_ref[:, j * LANES:(j + 1) * LANES]

    gpt = LANES // S5_GROUP
    spt = LANES // S5_STATE
    for j in range(nt):
        k = (j * spt) // gpt
        uk = ub[:, k * LANES:(k + 1) * LANES]
        s_ref[j] = jnp.dot(uk, bre_ref[j], preferred_element_type=F32)
        s_ref[nt + j] = jnp.dot(uk, bim_ref[j], preferred_element_type=F32)

    def step(t, carry):
        new = []
        for j in range(nt):
            cr, ci = carry[2 * j], carry[2 * j + 1]
            ar = ar_ref[:, j * LANES:(j + 1) * LANES]
            ai = ai_ref[:, j * LANES:(j + 1) * LANES]
            idx = pl.ds(pl.multiple_of(t * nb, nb), nb)
            nr = ar * cr - ai * ci + s_ref[j, idx, :]
            ni = ar * ci + ai * cr + s_ref[nt + j, idx, :]
            s_ref[j, idx, :] = nr
            s_ref[nt + j, idx, :] = ni
            new += [nr, ni]
        return tuple(new)

    init = []
    for j in range(nt):
        init += [carry_ref[j], carry_ref[nt + j]]
    fin = lax.fori_loop(0, lc, step, tuple(init))
    for j in range(nt):
        carry_ref[j] = fin[2 * j]
        carry_ref[nt + j] = fin[2 * j + 1]
        sr_ref[:, j * LANES:(j + 1) * LANES] = fin[2 * j]
        si_ref[:, j * LANES:(j + 1) * LANES] = fin[2 * j + 1]

    tpo = gpt // spt
    ys = []
    for m in range(u.shape[1] // LANES):
        acc = d_ref[:, m * LANES:(m + 1) * LANES] * u[:, m * LANES:(m + 1) * LANES]
        for k in range(tpo):
            j = tpo * m + k
            acc = acc + jnp.dot(s_ref[j].astype(BF16), cre_ref[j], preferred_element_type=F32)
            acc = acc - jnp.dot(s_ref[nt + j].astype(BF16), cim_ref[j], preferred_element_type=F32)
        ys.append(_gelu(acc))

    for k in range(nu):
        utm_ref[k] = ys[k]

    def to_batch_major(t, _):
        src = pl.ds(pl.multiple_of(t * nb, nb), nb)
        for k in range(nu):
            ubm_ref[k, pl.ds(t, nb, stride=lc), :] = utm_ref[k, src, :]
        return 0

    lax.fori_loop(0, lc, to_batch_major, 0)
    y = jnp.concatenate([ubm_ref[k] for k in range(nu)], axis=1)
    y = y * jax.nn.sigmoid(_bdot(y, wglu_ref[...]) + bglu_ref[...])
    o_ref[...] = _bdot(y, wa_ref[...]).reshape(o_ref.shape)


def _s5_prep(a_re, a_im, log_dt, b_re, b_im, c_re, c_im):
    g, p = a_re.shape
    h = b_re.shape[-1]
    dt = jnp.exp(log_dt)[:, None]
    mag = jnp.exp(dt * a_re)
    abar_re = mag * jnp.cos(dt * a_im)
    abar_im = mag * jnp.sin(dt * a_im)
    den = a_re * a_re + a_im * a_im
    nr = abar_re - 1.0
    f_re = (nr * a_re + abar_im * a_im) / den
    f_im = (abar_im * a_re - nr * a_im) / den
    bbar_re = f_re[..., None] * b_re - f_im[..., None] * b_im
    bbar_im = f_re[..., None] * b_im + f_im[..., None] * b_re
    gpt = LANES // h
    spt = LANES // p
    nt = g // spt
    place = (np.arange(g).reshape(nt, spt, 1) % gpt == np.arange(gpt)).astype(np.float32)

    def in_blocks(bbar):
        return jnp.einsum('jsph,jsl->jlhsp', bbar.reshape(nt, spt, p, h), place).reshape(nt, LANES, LANES).astype(BF16)

    def out_blocks(cm):
        return jnp.einsum('jshp,jsl->jsplh', cm.reshape(nt, spt, h, p), place).reshape(nt, LANES, LANES).astype(BF16)

    return (abar_re.reshape(1, g * p), abar_im.reshape(1, g * p), in_blocks(bbar_re), in_blocks(bbar_im),
            out_blocks(c_re), out_blocks(c_im))


def _s5_branch(proj3, col_blk, s0r, s0i, prep, d_flat, w_glu, b_glu, w_a, nb, lc):
    abar_re, abar_im, bre, bim, cre, cim = prep
    b3, l3, _ = proj3.shape
    width = w_glu.shape[0]
    dm = w_a.shape[1]
    ns = abar_re.shape[1]
    nt = ns // LANES
    blk_b = b3
    blk_l = nb * lc // b3
    nchunk = l3 // blk_l
    const2 = lambda c: (0, 0)
    const3 = lambda c: (0, 0, 0)
    kern = functools.partial(_s5_kernel, nb=nb, lc=lc)
    return pl.pallas_call(
        kern,
        grid=(nchunk,),
        in_specs=[pl.BlockSpec((blk_b, blk_l, width), lambda c: (0, c, col_blk)),
                  pl.BlockSpec((nb, ns), const2), pl.BlockSpec((nb, ns), const2),
                  pl.BlockSpec((1, ns), const2), pl.BlockSpec((1, ns), const2),
                  pl.BlockSpec((nt, LANES, LANES), const3), pl.BlockSpec((nt, LANES, LANES), const3),
                  pl.BlockSpec((nt, LANES, LANES), const3), pl.BlockSpec((nt, LANES, LANES), const3),
                  pl.BlockSpec((1, width), const2),
                  pl.BlockSpec((width, width), const2), pl.BlockSpec((1, width), const2),
                  pl.BlockSpec((width, dm), const2)],
        out_specs=[pl.BlockSpec((blk_b, blk_l, dm), lambda c: (0, c, 0)),
                   pl.BlockSpec((nb, ns), const2), pl.BlockSpec((nb, ns), const2)],
        out_shape=[jax.ShapeDtypeStruct((b3, l3, dm), F32),
                   jax.ShapeDtypeStruct((nb, ns), F32), jax.ShapeDtypeStruct((nb, ns), F32)],
        scratch_shapes=[pltpu.VMEM((2 * nt, nb * lc, LANES), F32),
                        pltpu.VMEM((2 * nt, nb, LANES), F32),
                        pltpu.VMEM((width // LANES, nb * lc, LANES), F32),
                        pltpu.VMEM((width // LANES, nb * lc, LANES), F32)],
        compiler_params=_params("arbitrary"),
        name="s5_branch",
    )(proj3, s0r, s0i, abar_re, abar_im, bre, bim, cre, cim, d_flat,
      w_glu.astype(BF16), b_glu.reshape(1, width), w_a.astype(BF16))


def _head_expand(nh, width):
    r = lax.broadcasted_iota(jnp.int32, (LANES, width), 0)
    c = lax.broadcasted_iota(jnp.int32, (LANES, width), 1)
    return jnp.where((c // (width // nh) == r) & (r < nh), 1.0, 0.0).astype(F32)


def _gated_norm_proj(y, z, gn, wb):
    y = y * jax.nn.silu(z)
    gw = y.shape[1] // M2_GROUPS
    parts = []
    for g in range(M2_GROUPS):
        yg = y[:, g * gw:(g + 1) * gw]
        parts.append(yg * lax.rsqrt(jnp.mean(yg * yg, axis=-1, keepdims=True) + GNORM_EPS))
    y = jnp.concatenate(parts, axis=1) * gn
    return _bdot(y, wb)


def _ssd_kernel(p_ref, pk_ref, wb_ref, o_ref, hout_ref, conv_ref, xp_ref, h_ref, *, nh):
    q = M2_CHUNK
    c = pl.program_id(1)
    inner = wb_ref.shape[0]
    cdim = xp_ref.shape[1]
    hd = inner // nh
    cw_ref = pk_ref
    cb_ref = pk_ref.at[M2_CONV:M2_CONV + 1]
    dsk_ref = pk_ref.at[M2_CONV + 1:M2_CONV + 2, 0:inner]
    gn_ref = pk_ref.at[M2_CONV + 1:M2_CONV + 2, inner:2 * inner]
    dtb_ref = pk_ref.at[M2_CONV + 2:M2_CONV + 3, 0:LANES]
    alog_ref = pk_ref.at[M2_CONV + 2:M2_CONV + 3, LANES:2 * LANES]
    hpg = nh // M2_GROUPS
    ns = M2_STATE

    @pl.when(c == 0)
    def _():
        xp_ref[0:8] = jnp.zeros((8, xp_ref.shape[1]), F32)
        h_ref[...] = jnp.zeros(h_ref.shape, F32)

    xp_ref[8:8 + q] = p_ref[0, :, 0:cdim]
    acc = cb_ref[...] + xp_ref[5:5 + q] * cw_ref[0:1, :]
    for k in range(1, M2_CONV):
        acc = acc + xp_ref[5 + k:5 + k + q] * cw_ref[k:k + 1, :]
    xbc = jax.nn.silu(acc)
    tail = xp_ref[5 + q:8 + q]
    conv_ref[0] = tail
    xp_ref[5:8] = tail

    xs = xbc[:, :inner]
    bm = xbc[:, inner:inner + M2_GROUPS * ns]
    cm = xbc[:, inner + M2_GROUPS * ns:]

    dt = _softplus(p_ref[0, :, cdim + inner:cdim + inner + LANES] + dtb_ref[...])
    da = dt * (-jnp.exp(alog_ref[...]))
    ii = lax.broadcasted_iota(jnp.int32, (q, q), 0)
    jj = lax.broadcasted_iota(jnp.int32, (q, q), 1)
    lower = ii >= jj
    cs = jnp.dot(jnp.where(lower, 1.0, 0.0).astype(F32), da, precision=HIGHEST, preferred_element_type=F32)
    cst = cs.T
    cs_last = cs[q - 1:q, :]
    expand = _head_expand(nh, inner)
    dt_full = jnp.dot(dt, expand, precision=HIGHEST, preferred_element_type=F32)
    dst_full = jnp.dot(jnp.exp(cs_last - cs), expand, precision=HIGHEST, preferred_element_type=F32)
    ecs_full = jnp.dot(jnp.exp(cs), expand, precision=HIGHEST, preferred_element_type=F32)
    xdt = xs * dt_full
    xwt = (xdt * dst_full).T
    lane = lax.broadcasted_iota(jnp.int32, (q, 2 * hd), 1)

    ys = []
    for g in range(M2_GROUPS):
        bg = bm[:, g * ns:(g + 1) * ns]
        cg = cm[:, g * ns:(g + 1) * ns].astype(BF16)
        cb = lax.dot_general(cg, bg.astype(BF16), (((1,), (1,)), ((), ())), preferred_element_type=F32)
        hprev = h_ref[g * hpg:(g + 1) * hpg].reshape(hpg * hd, ns)
        yoff = lax.dot_general(cg, hprev.astype(BF16), (((1,), (1,)), ((), ())), preferred_element_type=F32)
        yoff = yoff * ecs_full[:, g * hpg * hd:(g + 1) * hpg * hd]
        for pr in range(hpg // 2):
            h0 = g * hpg + 2 * pr
            ms = []
            for h in (h0, h0 + 1):
                seg = cs[:, h:h + 1] - cst[h:h + 1, :]
                ms.append((cb * jnp.exp(jnp.where(lower, seg, -jnp.inf))).astype(BF16))
            xp = xdt[:, h0 * hd:(h0 + 2) * hd]
            rhs = jnp.concatenate([jnp.where(lane < hd, xp, 0.0), jnp.where(lane >= hd, xp, 0.0)], axis=0)
            ydiag = jnp.dot(jnp.concatenate(ms, axis=1), rhs.astype(BF16), preferred_element_type=F32)
            ys.append(ydiag + yoff[:, 2 * pr * hd:(2 * pr + 2) * hd])
        st = jnp.dot(xwt[g * hpg * hd:(g + 1) * hpg * hd, :].astype(BF16), bg.astype(BF16),
                     preferred_element_type=F32)
        for hh in range(hpg):
            h = g * hpg + hh
            dec = jnp.exp(cst[h:h + 1, q - 1:q])
            h_ref[h] = dec * h_ref[h] + st[hh * hd:(hh + 1) * hd]
    hout_ref[0] = h_ref[...]
    y = jnp.concatenate(ys, axis=1) + dsk_ref[...] * xs
    o_ref[0] = _gated_norm_proj(y, p_ref[0, :, cdim:cdim + inner], gn_ref[...], wb_ref[...])


def _pad_lanes(v):
    return jnp.pad(v.reshape(1, -1), ((0, 0), (0, LANES - v.shape[-1])))


def _ssd_params(conv_w, conv_b, dt_bias, a_log, d_skip, g_norm):
    cdim = conv_w.shape[1]
    inner = g_norm.shape[0]
    hd = inner // dt_bias.shape[0]
    row5 = jnp.concatenate([jnp.repeat(d_skip, hd), g_norm]).reshape(1, -1)
    row6 = jnp.concatenate([_pad_lanes(dt_bias), _pad_lanes(a_log)], axis=1)
    pad = lambda r: jnp.pad(r, ((0, 0), (0, cdim - r.shape[1])))
    return jnp.concatenate([conv_w, conv_b.reshape(1, cdim), pad(row5), pad(row6), jnp.zeros((1, cdim), F32)], axis=0)


def _ssd_branch(proj3, pk, nh, w_b, pwidth):
    b, l, _ = proj3.shape
    inner, dm = w_b.shape
    hd = inner // nh
    cdim = pk.shape[1]
    q = M2_CHUNK
    c2 = lambda i, c: (0, 0)
    kern = functools.partial(_ssd_kernel, nh=nh)
    return pl.pallas_call(
        kern,
        grid=(b, l // q),
        in_specs=[pl.BlockSpec((1, q, pwidth), lambda i, c: (i, c, 0)),
                  pl.BlockSpec(pk.shape, c2),
                  pl.BlockSpec((inner, dm), c2)],
        out_specs=[pl.BlockSpec((1, q, dm), lambda i, c: (i, c, 0)),
                   pl.BlockSpec((1, nh, hd, M2_STATE), lambda i, c: (i, 0, 0, 0)),
                   pl.BlockSpec((1, M2_CONV - 1, cdim), lambda i, c: (i, 0, 0))],
        out_shape=[jax.ShapeDtypeStruct((b, l, dm), F32),
                   jax.ShapeDtypeStruct((b, nh, hd, M2_STATE), F32),
                   jax.ShapeDtypeStruct((b, M2_CONV - 1, cdim), F32)],
        scratch_shapes=[pltpu.VMEM((8 + q, cdim), F32), pltpu.VMEM((nh, hd, M2_STATE), F32)],
        compiler_params=_params("arbitrary", "arbitrary"),
        name="ssd_branch",
    )(proj3, pk, w_b.astype(BF16))


def _ssd_step_prep_kernel(xbc_ref, buf_ref, dtr_ref, cw_ref, cb_ref, dtb_ref, alog_ref,
                          xbc_o, buf_o, xdt_o, da_o, *, nh, inner):
    cdim = xbc_ref.shape[1]
    raw = xbc_ref[...]
    acc = cb_ref[...] + buf_ref[:, 0:cdim] * cw_ref[0:1, :]
    for k in range(1, M2_CONV - 1):
        acc = acc + buf_ref[:, k * cdim:(k + 1) * cdim] * cw_ref[k:k + 1, :]
    acc = acc + raw * cw_ref[M2_CONV - 1:M2_CONV, :]
    xbc = jax.nn.silu(acc)
    xbc_o[...] = xbc
    for k in range(M2_CONV - 2):
        buf_o[:, k * cdim:(k + 1) * cdim] = buf_ref[:, (k + 1) * cdim:(k + 2) * cdim]
    buf_o[:, (M2_CONV - 2) * cdim:] = raw
    dt = _softplus(dtr_ref[...] + dtb_ref[...])
    da = jnp.exp(dt * (-jnp.exp(alog_ref[...])))
    xdt_o[...] = xbc[:, :inner] * jnp.dot(dt, _head_expand(nh, inner), precision=HIGHEST,
                                         preferred_element_type=F32)
    da_o[...] = jnp.dot(da, _head_expand(nh, nh * LANES), precision=HIGHEST, preferred_element_type=F32)


def _ssd_step_state_kernel(h_ref, xdt_ref, ex_ref, ext_ref, b_ref, c_ref, da_ref, hn_ref, y_ref, *, hd):
    h = pl.program_id(0)
    xrep = jnp.dot(xdt_ref[...], ex_ref[0], precision=HIGHEST, preferred_element_type=F32)
    hn = jnp.tile(da_ref[...], (1, hd)) * h_ref[...] + xrep * jnp.tile(b_ref[...], (1, hd))
    hn_ref[...] = hn
    contrib = _bdot(hn * jnp.tile(c_ref[...], (1, hd)), ext_ref[0])

    @pl.when(h % 2 == 0)
    def _():
        y_ref[...] = contrib

    @pl.when(h % 2 == 1)
    def _():
        y_ref[...] += contrib


def _ssd_epilogue_kernel(y_ref, xbc_ref, z_ref, dsk_ref, gn_ref, wb_ref, o_ref):
    inner = y_ref.shape[1]
    y = y_ref[...] + dsk_ref[...] * xbc_ref[:, :inner]
    o_ref[...] = _gated_norm_proj(y, z_ref[...], gn_ref[...], wb_ref[...])


def _ssd_step(proj, state, conv_buf, conv_w, conv_b, dt_bias, a_log, d_skip, g_norm, w_b,
              col_xbc, col_z, col_dt):
    nb = proj.shape[0]
    nh = dt_bias.shape[0]
    inner = g_norm.shape[0]
    hd = inner // nh
    ns = M2_STATE
    cdim = conv_w.shape[1]
    dm = w_b.shape[1]
    nbuf = M2_CONV - 1
    c1 = lambda i: (0, 0)
    prep = functools.partial(_ssd_step_prep_kernel, nh=nh, inner=inner)
    xbc, new_buf, xdt, da_full = pl.pallas_call(
        prep,
        grid=(1,),
        in_specs=[pl.BlockSpec((nb, cdim), lambda i: (0, col_xbc)),
                  pl.BlockSpec((nb, nbuf * cdim), c1),
                  pl.BlockSpec((nb, LANES), lambda i: (0, col_dt)),
                  pl.BlockSpec((M2_CONV, cdim), c1), pl.BlockSpec((1, cdim), c1),
                  pl.BlockSpec((1, LANES), c1), pl.BlockSpec((1, LANES), c1)],
        out_specs=[pl.BlockSpec((nb, cdim), c1), pl.BlockSpec((nb, nbuf * cdim), c1),
                   pl.BlockSpec((nb, inner), c1), pl.BlockSpec((nb, nh * LANES), c1)],
        out_shape=[jax.ShapeDtypeStruct((nb, cdim), F32), jax.ShapeDtypeStruct((nb, nbuf * cdim), F32),
                   jax.ShapeDtypeStruct((nb, inner), F32), jax.ShapeDtypeStruct((nb, nh * LANES), F32)],
        compiler_params=_params("arbitrary"),
        name="ssd_step_prep",
    )(proj, conv_buf.reshape(nb, nbuf * cdim), proj, conv_w, conv_b.reshape(1, cdim),
      _pad_lanes(dt_bias), _pad_lanes(a_log))

    row = jnp.arange(2 * hd)[:, None]
    col = jnp.arange(hd * ns)[None, :]
    ex = jnp.stack([(row == col // ns), (row == col // ns + hd)]).astype(F32)
    ext = jnp.swapaxes(ex, 1, 2).astype(BF16)
    gcol = inner // ns
    hpg = nh // M2_GROUPS
    kern = functools.partial(_ssd_step_state_kernel, hd=hd)
    hn, y = pl.pallas_call(
        kern,
        grid=(nh,),
        in_specs=[pl.BlockSpec((nb, hd * ns), lambda h: (0, h)),
                  pl.BlockSpec((nb, 2 * hd), lambda h: (0, h // 2)),
                  pl.BlockSpec((1, 2 * hd, hd * ns), lambda h: (h % 2, 0, 0)),
                  pl.BlockSpec((1, hd * ns, 2 * hd), lambda h: (h % 2, 0, 0)),
                  pl.BlockSpec((nb, ns), lambda h: (0, gcol + h // hpg)),
                  pl.BlockSpec((nb, ns), lambda h: (0, gcol + M2_GROUPS + h // hpg)),
                  pl.BlockSpec((nb, LANES), lambda h: (0, h))],
        out_specs=[pl.BlockSpec((nb, hd * ns), lambda h: (0, h)),
                   pl.BlockSpec((nb, 2 * hd), lambda h: (0, h // 2))],
        out_shape=[jax.ShapeDtypeStruct((nb, nh * hd * ns), F32), jax.ShapeDtypeStruct((nb, inner), F32)],
        compiler_params=_params("arbitrary"),
        name="ssd_step_state",
    )(state.reshape(nb, nh * hd * ns), xdt, ex, ext, xbc, xbc, da_full)

    b_out = pl.pallas_call(
        _ssd_epilogue_kernel,
        grid=(1,),
        in_specs=[pl.BlockSpec((nb, inner), c1), pl.BlockSpec((nb, cdim), c1),
                  pl.BlockSpec((nb, inner), lambda i: (0, col_z)),
                  pl.BlockSpec((1, inner), c1), pl.BlockSpec((1, inner), c1),
                  pl.BlockSpec((inner, dm), c1)],
        out_specs=pl.BlockSpec((nb, dm), c1),
        out_shape=jax.ShapeDtypeStruct((nb, dm), F32),
        compiler_params=_params("arbitrary"),
        name="ssd_step_epilogue",
    )(y, xbc, proj, jnp.repeat(d_skip, hd).reshape(1, inner), g_norm.reshape(1, inner), w_b.astype(BF16))
    return b_out, hn.reshape(nb, nh, hd, ns), new_buf.reshape(nb, nbuf, cdim)


def _merge_kernel(x_ref, a_ref, b_ref, ga_ref, gb_ref, gt_ref, sh_ref, sc_ref, g_ref, w_ref, x1_ref, h2t_ref):
    merged = jax.nn.sigmoid(ga_ref[...]) * a_ref[...] + jax.nn.sigmoid(gb_ref[...]) * b_ref[...]
    x1 = x_ref[...] + _rows(gt_ref) * _bdot(merged, w_ref[...])
    x1_ref[...] = x1
    h2 = _rmsnorm(x1, g_ref[...]) * (1.0 + _rows(sc_ref)) + _rows(sh_ref)
    h2t_ref[...] = h2.T.astype(BF16)


def _merge(x2d, a2d, b2d, proj, mod, g2, w_out, col_ga, col_gb, tt):
    t, d = x2d.shape
    tile_of = lambda i: i
    row = lambda i: (i, 0)
    return pl.pallas_call(
        _merge_kernel,
        grid=(t // tt,),
        in_specs=[pl.BlockSpec((tt, d), row), pl.BlockSpec((tt, d), row), pl.BlockSpec((tt, d), row),
                  pl.BlockSpec((tt, d), lambda i: (i, col_ga)), pl.BlockSpec((tt, d), lambda i: (i, col_gb)),
                  mod.spec(2, tile_of), mod.spec(3, tile_of), mod.spec(4, tile_of),
                  pl.BlockSpec((1, d), lambda i: (0, 0)), pl.BlockSpec((d, d), lambda i: (0, 0))],
        out_specs=[pl.BlockSpec((tt, d), row), pl.BlockSpec((d, tt), lambda i: (0, i))],
        out_shape=[jax.ShapeDtypeStruct((t, d), F32), jax.ShapeDtypeStruct((d, t), BF16)],
        compiler_params=_params("arbitrary"),
        name="merge",
    )(x2d, a2d, b2d, proj, proj, mod.arr, mod.arr, mod.arr, g2.reshape(1, d), w_out.astype(BF16))


def _top_pairs(n):
    return [(i, j) for i in range(1, n + 1) for j in range(1, n // i + 1)]


def _rot_tile(lt, k, tt):
    p = (lt + k) % (tt // LANES)
    return slice(p * LANES, (p + 1) * LANES)


def _route_kernel(h2t_ref, wq_ref, keys_ref, r_ref, sc_ref, top_ref, cand_ref):
    c_ref, f_ref, s1_ref, e1_ref = (r_ref.at[k * PEER_HEADS:(k + 1) * PEER_HEADS] for k in range(4))
    ntop = PEER_TOPK + 1
    tt = h2t_ref.shape[1]
    qt = jnp.dot(wq_ref[...], h2t_ref[...], preferred_element_type=F32)
    dh = qt.shape[0] // (2 * PEER_HEADS)
    neg = jnp.full((1, tt), -jnp.inf, F32)

    def extract(s, ref, n):
        for r in range(n):
            m = jnp.max(s, axis=0, keepdims=True)
            ref[r:r + 1, :] = m
            if r < n - 1:
                s = jnp.where(s >= m, -jnp.inf, s)

    for hs in range(2 * PEER_HEADS):
        s = jnp.dot(keys_ref[hs], qt[hs * dh:(hs + 1) * dh].astype(BF16), preferred_element_type=F32)
        sc_ref[hs] = s
        extract(s, top_ref.at[hs], ntop)

    pairs = _top_pairs(ntop)
    npad = cand_ref.shape[0]
    for h in range(PEER_HEADS):
        ta, tb = top_ref.at[2 * h], top_ref.at[2 * h + 1]
        for idx, (i, j) in enumerate(pairs):
            cand_ref[idx:idx + 1, :] = ta[i - 1:i, :] + tb[j - 1:j, :]
        for idx in range(len(pairs), npad):
            cand_ref[idx:idx + 1, :] = neg
        extract(cand_ref[...], top_ref.at[2 * PEER_HEADS], ntop)
        best = top_ref.at[2 * PEER_HEADS]
        v1 = best[0:1, :]
        z = jnp.zeros((1, tt), F32)
        for r in range(PEER_TOPK):
            z = z + jnp.exp(best[r:r + 1, :] - v1)
        thr = 0.5 * (best[PEER_TOPK - 1:PEER_TOPK, :] + best[PEER_TOPK:PEER_TOPK + 1, :])
        s0 = sc_ref[2 * h]
        s1 = sc_ref[2 * h + 1]
        c_ref[h] = thr - s0
        f_ref[h] = jnp.exp(s0 - ta[0:1, :]) * (0.5 / z)
        e1 = jnp.exp(s1 - tb[0:1, :])
        for lt in range(tt // LANES):
            src = slice(lt * LANES, (lt + 1) * LANES)
            s1_ref[h, :, _rot_tile(lt, 2 * h, tt)] = s1[:, src]
            e1_ref[h, :, _rot_tile(lt, 2 * h + 1, tt)] = e1[:, src]


def _route(h2t, wq_t, keys, tt):
    d, t = h2t.shape
    nq = wq_t.shape[0]
    nhs = keys.shape[0]
    ntop = PEER_TOPK + 1
    npad = -(-len(_top_pairs(ntop)) // 8) * 8
    blk = pl.BlockSpec((4 * PEER_HEADS, PEER_NKEYS, tt), lambda i: (0, 0, i))
    shp = jax.ShapeDtypeStruct((4 * PEER_HEADS, PEER_NKEYS, t), F32)
    return pl.pallas_call(
        _route_kernel,
        grid=(t // tt,),
        in_specs=[pl.BlockSpec((d, tt), lambda i: (0, i)),
                  pl.BlockSpec((nq, d), lambda i: (0, 0)),
                  pl.BlockSpec(keys.shape, lambda i: (0, 0, 0))],
        out_specs=blk,
        out_shape=shp,
        scratch_shapes=[pltpu.VMEM((nhs, PEER_NKEYS, tt), F32),
                        pltpu.VMEM((nhs + 1, 24, tt), F32),
                        pltpu.VMEM((npad, tt), F32)],
        compiler_params=_params("arbitrary"),
        name="peer_route",
    )(h2t, wq_t, keys)


def _tree_sum(xs):
    while len(xs) > 1:
        xs = [xs[k] + xs[k + 1] for k in range(0, len(xs) - 1, 2)] + ([xs[-1]] if len(xs) % 2 else [])
    return xs[0]


def _peer_kernel(h2t_ref, down_ref, upt_ref, r_ref, x1_ref, pm_ref, y_ref, acc_ref, d0_ref, d1_ref, a0_ref, a1_ref,
                 *, final, ne, nsteps):
    g = pl.program_id(0)
    et, tt = d0_ref.shape
    rps = et // PEER_NKEYS
    c_ref, f_ref, s1_ref, e1_ref = (r_ref.at[k * PEER_HEADS:(k + 1) * PEER_HEADS] for k in range(4))
    dm = acc_ref.shape[0]
    t1 = jnp.clip(g - 1, 0, nsteps - 1)
    t2 = g - 2

    @pl.when(g == 0)
    def _():
        d1_ref[...] = jnp.zeros(d1_ref.shape, F32)
        a0_ref[...] = jnp.zeros(a0_ref.shape, BF16)
        a1_ref[...] = jnp.zeros(a1_ref.shape, BF16)

    @pl.when((g == 0) | ((t2 >= 0) & (t2 % ne == 0)))
    def _():
        acc_ref[...] = jnp.zeros(acc_ref.shape, F32)

    half = 8
    nlt = tt // LANES
    n_down, n_up = et // LANES, dm // LANES

    def stages(d_new, d_cur, a_new, a_cur):
        def down_chunk(ck):
            rows = slice(ck * (et // n_down), (ck + 1) * (et // n_down))
            d_new[rows, :] = jnp.dot(down_ref[rows, :], h2t_ref[...], preferred_element_type=F32)

        def up_chunk(ck):
            rows = slice(ck * (dm // n_up), (ck + 1) * (dm // n_up))
            acc_ref[rows, :] += jnp.dot(upt_ref[rows, :], a_cur[...], preferred_element_type=F32)

        def weights_block(r, lt):
            i = (t1 % ne) * rps + r
            ls = slice(lt * LANES, (lt + 1) * LANES)
            cs = [jnp.broadcast_to(c_ref[h, pl.ds(i, 1), :][:, ls], (half, LANES)) for h in range(PEER_HEADS)]
            fs = [jnp.broadcast_to(f_ref[h, pl.ds(i, 1), :][:, ls], (half, LANES)) for h in range(PEER_HEADS)]
            for sb in range(PEER_NKEYS // (2 * half)):
                blk = []
                for hf in range(2):
                    j0 = sb * 2 * half + hf * half
                    js = slice(j0, j0 + half)
                    w = _tree_sum([jnp.where(s1_ref[h, js, _rot_tile(lt, 2 * h, tt)] >= cs[h],
                                             e1_ref[h, js, _rot_tile(lt, 2 * h + 1, tt)], 0.0) * fs[h]
                                   for h in range(PEER_HEADS)])
                    x = d_cur[r * PEER_NKEYS + j0:r * PEER_NKEYS + j0 + half, ls]
                    blk.append(x * (1.0 + lax.erf(x * (1.0 / math.sqrt(2.0)))) * w)
                a_new[r * PEER_NKEYS + sb * 2 * half:r * PEER_NKEYS + (sb + 1) * 2 * half, ls] = (
                    jnp.concatenate(blk, axis=0).astype(BF16))

        units = []
        k1 = k2 = 0
        while k1 < n_down or k2 < n_up:
            if k1 < n_down and k1 * n_up <= k2 * n_down:
                units.append((down_chunk, k1))
                k1 += 1
            else:
                units.append((up_chunk, k2))
                k2 += 1
        blocks = [(r, lt) for r in range(rps) for lt in range(nlt)]
        for k, (r, lt) in enumerate(blocks):
            for u, (fn, ck) in enumerate(units):
                if (u * len(blocks)) // len(units) == k:
                    fn(ck)
            weights_block(r, lt)

    @pl.when(g % 2 == 0)
    def _():
        stages(d0_ref, d1_ref, a1_ref, a0_ref)

    @pl.when(g % 2 == 1)
    def _():
        stages(d1_ref, d0_ref, a0_ref, a1_ref)

    @pl.when((t2 >= 0) & (t2 % ne == ne - 1))
    def _():
        pm = _rows(pm_ref)
        x2 = x1_ref[...] + pm[:, :dm] * acc_ref[...].T
        if final:
            x2 = _rmsnorm(x2, pm[:, 3 * dm:]) * (1.0 + pm[:, 2 * dm:3 * dm]) + pm[:, dm:2 * dm]
        y_ref[...] = x2


def _peer(h2t, down, up_t3, route, x1, pm, tt, final):
    d, t = h2t.shape
    ne, _, et = up_t3.shape
    nt = t // tt
    nsteps = nt * ne
    lag = lambda g, k: jnp.clip(g - k, 0, nsteps - 1)
    tok = lambda g, k: lag(g, k) // ne
    exp = lambda g, k: lag(g, k) % ne
    tile_of = lambda g: tok(g, 2)
    rblk = pl.BlockSpec((4 * PEER_HEADS, PEER_NKEYS, tt), lambda g: (0, 0, tok(g, 1)))
    return pl.pallas_call(
        functools.partial(_peer_kernel, final=final, ne=ne, nsteps=nsteps),
        grid=(nsteps + 2,),
        in_specs=[pl.BlockSpec((d, tt), lambda g: (0, tok(g, 0))),
                  pl.BlockSpec((et, d), lambda g: (exp(g, 0), 0)),
                  pl.BlockSpec((None, d, et), lambda g: (exp(g, 2), 0, 0)),
                  rblk,
                  pl.BlockSpec((tt, d), lambda g: (tok(g, 2), 0)),
                  pm.spec(0, tile_of)],
        out_specs=pl.BlockSpec((tt, d), lambda g: (tok(g, 2), 0)),
        out_shape=jax.ShapeDtypeStruct((t, d), F32),
        scratch_shapes=[pltpu.VMEM((d, tt), F32), pltpu.VMEM((et, tt), F32), pltpu.VMEM((et, tt), F32),
                        pltpu.VMEM((et, tt), BF16), pltpu.VMEM((et, tt), BF16)],
        compiler_params=_params("arbitrary"),
        name="peer_experts",
    )(h2t, down, up_t3, route, x1, pm.arr)


def _reorder_w_in(w_in, s5w, inner, cdim, nh, pwidth):
    o1 = s5w
    o2 = o1 + inner
    o3 = o2 + cdim
    o4 = o3 + nh
    pad = jnp.zeros((w_in.shape[0], pwidth - (cdim + inner + nh)), w_in.dtype)
    return jnp.concatenate([w_in[:, o2:o3], w_in[:, o1:o2], w_in[:, o3:o4], pad, w_in[:, :o1], w_in[:, o4:]],
                           axis=1).astype(BF16)


def kernel(x_prompt, x_sample, state_s5_re, state_s5_im, state_ssm, state_conv, c_prompt, c_sample, w_ada, b_ada, g_norm1, g_norm2, w_in, s5_a_re, s5_a_im, s5_log_dt, s5_b_re, s5_b_im, s5_c_re, s5_c_im, s5_d, s5_w_glu, s5_b_glu, w_branch_a, conv_w, conv_b, dt_bias, a_log, d_skip, g_ssm_norm, w_branch_b, w_out, w_query, sub_keys, expert_down, expert_up, g_final, w_ada_final, b_ada_final):
    depth = w_ada.shape[0]
    bp, lp, d = x_prompt.shape
    bs = x_sample.shape[0]
    assert x_sample.shape[1] == 1 and lp % M2_CHUNK == 0
    g5, p5 = s5_a_re.shape[1:]
    s5w = g5 * S5_GROUP
    inner = g_ssm_norm.shape[1]
    nh = dt_bias.shape[1]
    cdim = conv_w.shape[2]
    tp = bp * lp

    pwidth = -(-(cdim + inner + LANES) // s5w) * s5w
    assert (pwidth + s5w) % d == 0 and cdim % inner == 0
    col_xbc, col_z, col_dt = 0, cdim // inner, (cdim + inner) // LANES
    col_u = pwidth // s5w
    col_ga = (pwidth + s5w) // d
    col_gb = col_ga + 1

    c_all = jnp.concatenate([c_prompt, c_sample], axis=0)
    mod_fin = _cond_proj(c_all, w_ada_final, b_ada_final)
    tt_p = 512
    et = 1024
    gfb = jnp.broadcast_to(g_final.reshape(1, d), (bp + bs, d))

    ne = expert_up.shape[1] // et
    down_all = expert_down.reshape(depth * ne * et, d).astype(BF16)
    up_all = jnp.swapaxes(expert_up.reshape(depth * ne, et, d), 1, 2).astype(BF16)

    xp = x_prompt.reshape(tp, d)
    xs = x_sample.reshape(bs, d)
    outs_p, outs_s = [], []
    for l in range(depth):
        mod_all = _cond_proj(c_all, w_ada[l], b_ada[l])
        mod_p = _Mod(mod_all[:bp], d, False, lp // tt_p)
        mod_s = _Mod(mod_all[bp:], d, True, 1)
        final = l == depth - 1
        w_in_r = _reorder_w_in(w_in[l], s5w, inner, cdim, nh, pwidth)
        pm_all = jnp.concatenate([mod_all[:, 5 * d:], mod_fin, gfb], axis=1)
        pm_p = _Mod(pm_all[:bp], 4 * d, False, lp // tt_p)
        pm_s = _Mod(pm_all[bp:], 4 * d, True, 1)
        pk = _ssd_params(conv_w[l], conv_b[l], dt_bias[l], a_log[l], d_skip[l], g_ssm_norm[l])
        s5p = _s5_prep(s5_a_re[l], s5_a_im[l], s5_log_dt[l], s5_b_re[l], s5_b_im[l], s5_c_re[l], s5_c_im[l])
        d5 = s5_d[l].reshape(1, s5w)
        wq_t = w_query[l].T.astype(BF16)
        keys = sub_keys[l].reshape(2 * PEER_HEADS, PEER_NKEYS, -1).astype(BF16)
        down = down_all[l * ne * et:(l + 1) * ne * et]
        up_t3 = up_all[l * ne:(l + 1) * ne]

        proj = _inproj(xp, g_norm1[l], mod_p, w_in_r, tt_p)
        proj3 = proj.reshape(bp, lp, -1)
        zeros5 = jnp.zeros((bp, g5 * p5), F32)
        a_out, sr, si = _s5_branch(proj3, col_u, zeros5, zeros5, s5p, d5, s5_w_glu[l], s5_b_glu[l],
                                   w_branch_a[l], bp, 128)
        b_out, hl, nbuf = _ssd_branch(proj3, pk, nh, w_branch_b[l], pwidth)
        x1, h2t = _merge(xp, a_out.reshape(tp, d), b_out.reshape(tp, d), proj, mod_p, g_norm2[l], w_out[l],
                         col_ga, col_gb, tt_p)
        route = _route(h2t, wq_t, keys, tt_p)
        xp = _peer(h2t, down, up_t3, route, x1, pm_p, tt_p, final)
        outs_p.append((sr.reshape(bp, g5, p5), si.reshape(bp, g5, p5), hl, nbuf))

        proj_s = _inproj(xs, g_norm1[l], mod_s, w_in_r, bs)
        a_s, sr_s, si_s = _s5_branch(proj_s.reshape(1, bs, -1), col_u, state_s5_re[l].reshape(bs, g5 * p5),
                                     state_s5_im[l].reshape(bs, g5 * p5), s5p, d5, s5_w_glu[l], s5_b_glu[l],
                                     w_branch_a[l], bs, 1)
        b_s, hl_s, nbuf_s = _ssd_step(proj_s, state_ssm[l], state_conv[l], conv_w[l], conv_b[l], dt_bias[l],
                                      a_log[l], d_skip[l], g_ssm_norm[l], w_branch_b[l], col_xbc, col_z, col_dt)
        x1_s, h2t_s = _merge(xs, a_s.reshape(bs, d), b_s, proj_s, mod_s, g_norm2[l], w_out[l], col_ga, col_gb, bs)
        route_s = _route(h2t_s, wq_t, keys, bs)
        xs = _peer(h2t_s, down, up_t3, route_s, x1_s, pm_s, bs, final)
        outs_s.append((sr_s.reshape(bs, g5, p5), si_s.reshape(bs, g5, p5), hl_s, nbuf_s))

    stack = lambda outs, k: jnp.stack([o[k] for o in outs])
    return (xp.reshape(bp, lp, d), xs.reshape(bs, 1, d),
            stack(outs_p, 0), stack(outs_p, 1), stack(outs_p, 2), stack(outs_p, 3),
            stack(outs_s, 0), stack(outs_s, 1), stack(outs_s, 2), stack(outs_s, 3))
```

```python
import functools
import math

import jax
import jax.numpy as jnp
import numpy as np
from jax import lax
from jax.experimental import pallas as pl
from jax.experimental.pallas import tpu as pltpu

F32 = jnp.float32
BF16 = jnp.bfloat16
HIGHEST = lax.Precision.HIGHEST

LANES = 128
NORM_EPS = 1e-6
GNORM_EPS = 1e-5

S5_GROUP = 16
S5_STATE = 64
M2_HEADDIM = 64
M2_GROUPS = 4
M2_STATE = 128
M2_CONV = 4
M2_CHUNK = 128
PEER_HEADS = 8
PEER_NKEYS = 128
PEER_TOPK = 16

VMEM_LIMIT = 56 * 1024 * 1024


def _params(*sem):
    return pltpu.CompilerParams(dimension_semantics=sem, vmem_limit_bytes=VMEM_LIMIT)


def _bdot(a, b):
    return jnp.dot(a.astype(BF16), b.astype(BF16), preferred_element_type=F32)


def _rmsnorm(x, g):
    return x * lax.rsqrt(jnp.mean(x * x, axis=-1, keepdims=True) + NORM_EPS) * g


def _rows(ref):
    v = ref[...]
    return v.reshape(-1, v.shape[-1])


def _gelu(x):
    return 0.5 * x * (1.0 + lax.erf(x * (1.0 / math.sqrt(2.0))))


def _softplus(x):
    return jnp.maximum(x, 0.0) + jnp.log1p(jnp.exp(-jnp.abs(x)))


def _cond_kernel(c_ref, w_ref, b_ref, o_ref):
    sc = jax.nn.silu(c_ref[...])
    o_ref[...] = jnp.dot(sc, w_ref[...], precision=HIGHEST, preferred_element_type=F32) + b_ref[...]


def _cond_proj(c, w, b):
    r, d = c.shape
    n = w.shape[1]
    tn = 1024
    return pl.pallas_call(
        _cond_kernel,
        grid=(n // tn,),
        in_specs=[pl.BlockSpec((r, d), lambda j: (0, 0)),
                  pl.BlockSpec((d, tn), lambda j: (0, j)),
                  pl.BlockSpec((1, tn), lambda j: (0, j))],
        out_specs=pl.BlockSpec((r, tn), lambda j: (0, j)),
        out_shape=jax.ShapeDtypeStruct((r, n), F32),
        compiler_params=_params("arbitrary"),
        name="cond_proj",
    )(c, w, b.reshape(1, n))


class _Mod:
    def __init__(self, mod, d, per_token, tiles_per_batch):
        self.per_token = per_token
        self.tpb = tiles_per_batch
        self.d = d
        self.arr = mod if per_token else mod.reshape(mod.shape[0], 1, mod.shape[1])

    def spec(self, k, tile_of):
        if self.per_token:
            return pl.BlockSpec((self.arr.shape[0], self.d), lambda *g: (0, k))
        return pl.BlockSpec((1, 1, self.d), lambda *g: (tile_of(*g) // self.tpb, 0, k))


def _inproj_kernel(x_ref, g_ref, sh_ref, sc_ref, w_ref, o_ref):
    h = _rmsnorm(x_ref[...], g_ref[...]) * (1.0 + _rows(sc_ref)) + _rows(sh_ref)
    o_ref[...] = _bdot(h, w_ref[...])


def _inproj(x2d, g, mod, w_bf16, tt):
    t, d = x2d.shape
    n = w_bf16.shape[1]
    tn = n // 3
    tile_of = lambda j, i: i
    return pl.pallas_call(
        _inproj_kernel,
        grid=(n // tn, t // tt),
        in_specs=[pl.BlockSpec((tt, d), lambda j, i: (i, 0)),
                  pl.BlockSpec((1, d), lambda j, i: (0, 0)),
                  mod.spec(0, tile_of), mod.spec(1, tile_of),
                  pl.BlockSpec((d, tn), lambda j, i: (0, j))],
        out_specs=pl.BlockSpec((tt, tn), lambda j, i: (i, j)),
        out_shape=jax.ShapeDtypeStruct((t, n), F32),
        compiler_params=_params("arbitrary", "arbitrary"),
        name="inproj",
    )(x2d, g.reshape(1, d), mod.arr, mod.arr, w_bf16)


def _s5_kernel(u_ref, s0r_ref, s0i_ref, ar_ref, ai_ref, bre_ref, bim_ref, cre_ref, cim_ref, d_ref,
               wglu_ref, bglu_ref, wa_ref, o_ref, sr_ref, si_ref, s_ref, carry_ref, ubm_ref, utm_ref, *, nb, lc):
    nt = ar_ref.shape[1] // LANES
    nu = ubm_ref.shape[0]
    c = pl.program_id(0)

    u_bm = _rows(u_ref)
    for k in range(nu):
        ubm_ref[k] = u_bm[:, k * LANES:(k + 1) * LANES]

    def to_time_major(t, _):
        dst = pl.ds(pl.multiple_of(t * nb, nb), nb)
        for k in range(nu):
            utm_ref[k, dst, :] = ubm_ref[k, pl.ds(t, nb, stride=lc), :]
        return 0

    lax.fori_loop(0, lc, to_time_major, 0)
    u = jnp.concatenate([utm_ref[k] for k in range(nu)], axis=1)
    ub = u.astype(BF16)

    @pl.when(c == 0)
    def _():
        for j in range(nt):
            carry_ref[j] = s0r_ref[:, j * LANES:(j + 1) * LANES]
            carry_ref[nt + j] = s0i_ref[:, j * LANES:(j + 1) * LANES]

    gpt = LANES // S5_GROUP
    spt = LANES // S5_STATE
    for j in range(nt):
        k = (j * spt) // gpt
        uk = ub[:, k * LANES:(k + 1) * LANES]
        s_ref[j] = jnp.dot(uk, bre_ref[j], preferred_element_type=F32)
        s_ref[nt + j] = jnp.dot(uk, bim_ref[j], preferred_element_type=F32)

    def step(t, carry):
        new = []
        for j in range(nt):
            cr, ci = carry[2 * j], carry[2 * j + 1]
            ar = ar_ref[:, j * LANES:(j + 1) * LANES]
            ai = ai_ref[:, j * LANES:(j + 1) * LANES]
            idx = pl.ds(pl.multiple_of(t * nb, nb), nb)
            nr = ar * cr - ai * ci + s_ref[j, idx, :]
            ni = ar * ci + ai * cr + s_ref[nt + j, idx, :]
            s_ref[j, idx, :] = nr
            s_ref[nt + j, idx, :] = ni
            new += [nr, ni]
        return tuple(new)

    init = []
    for j in range(nt):
        init += [carry_ref[j], carry_ref[nt + j]]
    fin = lax.fori_loop(0, lc, step, tuple(init))
    for j in range(nt):
        carry_ref[j] = fin[2 * j]
        carry_ref[nt + j] = fin[2 * j + 1]
        sr_ref[:, j * LANES:(j + 1) * LANES] = fin[2 * j]
        si_ref[:, j * LANES:(j + 1) * LANES] = fin[2 * j + 1]

    tpo = gpt // spt
    ys = []
    for m in range(u.shape[1] // LANES):
        acc = d_ref[:, m * LANES:(m + 1) * LANES] * u[:, m * LANES:(m + 1) * LANES]
        for k in range(tpo):
            j = tpo * m + k
            acc = acc + jnp.dot(s_ref[j].astype(BF16), cre_ref[j], preferred_element_type=F32)
            acc = acc - jnp.dot(s_ref[nt + j].astype(BF16), cim_ref[j], preferred_element_type=F32)
        ys.append(_gelu(acc))

    for k in range(nu):
        utm_ref[k] = ys[k]

    def to_batch_major(t, _):
        src = pl.ds(pl.multiple_of(t * nb, nb), nb)
        for k in range(nu):
            ubm_ref[k, pl.ds(t, nb, stride=lc), :] = utm_ref[k, src, :]
        return 0

    lax.fori_loop(0, lc, to_batch_major, 0)
    y = jnp.concatenate([ubm_ref[k] for k in range(nu)], axis=1)
    y = y * jax.nn.sigmoid(_bdot(y, wglu_ref[...]) + bglu_ref[...])
    o_ref[...] = _bdot(y, wa_ref[...]).reshape(o_ref.shape)


def _s5_prep(a_re, a_im, log_dt, b_re, b_im, c_re, c_im):
    g, p = a_re.shape
    h = b_re.shape[-1]
    dt = jnp.exp(log_dt)[:, None]
    mag = jnp.exp(dt * a_re)
    abar_re = mag * jnp.cos(dt * a_im)
    abar_im = mag * jnp.sin(dt * a_im)
    den = a_re * a_re + a_im * a_im
    nr = abar_re - 1.0
    f_re = (nr * a_re + abar_im * a_im) / den
    f_im = (abar_im * a_re - nr * a_im) / den
    bbar_re = f_re[..., None] * b_re - f_im[..., None] * b_im
    bbar_im = f_re[..., None] * b_im + f_im[..., None] * b_re
    gpt = LANES // h
    spt = LANES // p
    nt = g // spt
    place = (np.arange(g).reshape(nt, spt, 1) % gpt == np.arange(gpt)).astype(np.float32)

    def in_blocks(bbar):
        return jnp.einsum('jsph,jsl->jlhsp', bbar.reshape(nt, spt, p, h), place).reshape(nt, LANES, LANES).astype(BF16)

    def out_blocks(cm):
        return jnp.einsum('jshp,jsl->jsplh', cm.reshape(nt, spt, h, p), place).reshape(nt, LANES, LANES).astype(BF16)

    return (abar_re.reshape(1, g * p), abar_im.reshape(1, g * p), in_blocks(bbar_re), in_blocks(bbar_im),
            out_blocks(c_re), out_blocks(c_im))


def _s5_branch(proj3, col_blk, s0r, s0i, prep, d_flat, w_glu, b_glu, w_a, nb, lc):
    abar_re, abar_im, bre, bim, cre, cim = prep
    b3, l3, _ = proj3.shape
    width = w_glu.shape[0]
    dm = w_a.shape[1]
    ns = abar_re.shape[1]
    nt = ns // LANES
    blk_b = b3
    blk_l = nb * lc // b3
    nchunk = l3 // blk_l
    const2 = lambda c: (0, 0)
    const3 = lambda c: (0, 0, 0)
    kern = functools.partial(_s5_kernel, nb=nb, lc=lc)
    return pl.pallas_call(
        kern,
        grid=(nchunk,),
        in_specs=[pl.BlockSpec((blk_b, blk_l, width), lambda c: (0, c, col_blk)),
                  pl.BlockSpec((nb, ns), const2), pl.BlockSpec((nb, ns), const2),
                  pl.BlockSpec((1, ns), const2), pl.BlockSpec((1, ns), const2),
                  pl.BlockSpec((nt, LANES, LANES), const3), pl.BlockSpec((nt, LANES, LANES), const3),
                  pl.BlockSpec((nt, LANES, LANES), const3), pl.BlockSpec((nt, LANES, LANES), const3),
                  pl.BlockSpec((1, width), const2),
                  pl.BlockSpec((width, width), const2), pl.BlockSpec((1, width), const2),
                  pl.BlockSpec((width, dm), const2)],
        out_specs=[pl.BlockSpec((blk_b, blk_l, dm), lambda c: (0, c, 0)),
                   pl.BlockSpec((nb, ns), const2), pl.BlockSpec((nb, ns), const2)],
        out_shape=[jax.ShapeDtypeStruct((b3, l3, dm), F32),
                   jax.ShapeDtypeStruct((nb, ns), F32), jax.ShapeDtypeStruct((nb, ns), F32)],
        scratch_shapes=[pltpu.VMEM((2 * nt, nb * lc, LANES), F32),
                        pltpu.VMEM((2 * nt, nb, LANES), F32),
                        pltpu.VMEM((width // LANES, nb * lc, LANES), F32),
                        pltpu.VMEM((width // LANES, nb * lc, LANES), F32)],
        compiler_params=_params("arbitrary"),
        name="s5_branch",
    )(proj3, s0r, s0i, abar_re, abar_im, bre, bim, cre, cim, d_flat,
      w_glu.astype(BF16), b_glu.reshape(1, width), w_a.astype(BF16))


def _head_expand(nh, width):
    r = lax.broadcasted_iota(jnp.int32, (LANES, width), 0)
    c = lax.broadcasted_iota(jnp.int32, (LANES, width), 1)
    return jnp.where((c // (width // nh) == r) & (r < nh), 1.0, 0.0).astype(F32)


def _gated_norm_proj(y, z, gn, wb):
    y = y * jax.nn.silu(z)
    gw = y.shape[1] // M2_GROUPS
    parts = []
    for g in range(M2_GROUPS):
        yg = y[:, g * gw:(g + 1) * gw]
        parts.append(yg * lax.rsqrt(jnp.mean(yg * yg, axis=-1, keepdims=True) + GNORM_EPS))
    y = jnp.concatenate(parts, axis=1) * gn
    return _bdot(y, wb)


def _ssd_kernel(p_ref, pk_ref, wb_ref, o_ref, hout_ref, conv_ref, xp_ref, h_ref, *, nh):
    q = M2_CHUNK
    c = pl.program_id(1)
    inner = wb_ref.shape[0]
    cdim = xp_ref.shape[1]
    hd = inner // nh
    cw_ref = pk_ref
    cb_ref = pk_ref.at[M2_CONV:M2_CONV + 1]
    dsk_ref = pk_ref.at[M2_CONV + 1:M2_CONV + 2, 0:inner]
    gn_ref = pk_ref.at[M2_CONV + 1:M2_CONV + 2, inner:2 * inner]
    dtb_ref = pk_ref.at[M2_CONV + 2:M2_CONV + 3, 0:LANES]
    alog_ref = pk_ref.at[M2_CONV + 2:M2_CONV + 3, LANES:2 * LANES]
    hpg = nh // M2_GROUPS
    ns = M2_STATE

    @pl.when(c == 0)
    def _():
        xp_ref[0:8] = jnp.zeros((8, xp_ref.shape[1]), F32)
        h_ref[...] = jnp.zeros(h_ref.shape, F32)

    xp_ref[8:8 + q] = p_ref[0, :, 0:cdim]
    acc = cb_ref[...] + xp_ref[5:5 + q] * cw_ref[0:1, :]
    for k in range(1, M2_CONV):
        acc = acc + xp_ref[5 + k:5 + k + q] * cw_ref[k:k + 1, :]
    xbc = jax.nn.silu(acc)
    tail = xp_ref[5 + q:8 + q]
    conv_ref[0] = tail
    xp_ref[5:8] = tail

    xs = xbc[:, :inner]
    bm = xbc[:, inner:inner + M2_GROUPS * ns]
    cm = xbc[:, inner + M2_GROUPS * ns:]

    dt = _softplus(p_ref[0, :, cdim + inner:cdim + inner + LANES] + dtb_ref[...])
    da = dt * (-jnp.exp(alog_ref[...]))
    ii = lax.broadcasted_iota(jnp.int32, (q, q), 0)
    jj = lax.broadcasted_iota(jnp.int32, (q, q), 1)
    lower = ii >= jj
    cs = jnp.dot(jnp.where(lower, 1.0, 0.0).astype(F32), da, precision=HIGHEST, preferred_element_type=F32)
    cst = cs.T
    cs_last = cs[q - 1:q, :]
    expand = _head_expand(nh, inner)
    dt_full = jnp.dot(dt, expand, precision=HIGHEST, preferred_element_type=F32)
    dst_full = jnp.dot(jnp.exp(cs_last - cs), expand, precision=HIGHEST, preferred_element_type=F32)
    ecs_full = jnp.dot(jnp.exp(cs), expand, precision=HIGHEST, preferred_element_type=F32)
    xdt = xs * dt_full
    xwt = (xdt * dst_full).T
    lane = lax.broadcasted_iota(jnp.int32, (q, 2 * hd), 1)

    ys = []
    for g in range(M2_GROUPS):
        bg = bm[:, g * ns:(g + 1) * ns]
        cg = cm[:, g * ns:(g + 1) * ns].astype(BF16)
        cb = lax.dot_general(cg, bg.astype(BF16), (((1,), (1,)), ((), ())), preferred_element_type=F32)
        hprev = h_ref[g * hpg:(g + 1) * hpg].reshape(hpg * hd, ns)
        yoff = lax.dot_general(cg, hprev.astype(BF16), (((1,), (1,)), ((), ())), preferred_element_type=F32)
        yoff = yoff * ecs_full[:, g * hpg * hd:(g + 1) * hpg * hd]
        for pr in range(hpg // 2):
            h0 = g * hpg + 2 * pr
            ms = []
            for h in (h0, h0 + 1):
                seg = cs[:, h:h + 1] - cst[h:h + 1, :]
                ms.append((cb * jnp.exp(jnp.where(lower, seg, -jnp.inf))).astype(BF16))
            xp = xdt[:, h0 * hd:(h0 + 2) * hd]
            rhs = jnp.concatenate([jnp.where(lane < hd, xp, 0.0), jnp.where(lane >= hd, xp, 0.0)], axis=0)
            ydiag = jnp.dot(jnp.concatenate(ms, axis=1), rhs.astype(BF16), preferred_element_type=F32)
            ys.append(ydiag + yoff[:, 2 * pr * hd:(2 * pr + 2) * hd])
        st = jnp.dot(xwt[g * hpg * hd:(g + 1) * hpg * hd, :].astype(BF16), bg.astype(BF16),
                     preferred_element_type=F32)
        for hh in range(hpg):
            h = g * hpg + hh
            dec = jnp.exp(cst[h:h + 1, q - 1:q])
            h_ref[h] = dec * h_ref[h] + st[hh * hd:(hh + 1) * hd]
    hout_ref[0] = h_ref[...]
    y = jnp.concatenate(ys, axis=1) + dsk_ref[...] * xs
    o_ref[0] = _gated_norm_proj(y, p_ref[0, :, cdim:cdim + inner], gn_ref[...], wb_ref[...])


def _pad_lanes(v):
    return jnp.pad(v.reshape(1, -1), ((0, 0), (0, LANES - v.shape[-1])))


def _ssd_params(conv_w, conv_b, dt_bias, a_log, d_skip, g_norm):
    cdim = conv_w.shape[1]
    inner = g_norm.shape[0]
    hd = inner // dt_bias.shape[0]
    row5 = jnp.concatenate([jnp.repeat(d_skip, hd), g_norm]).reshape(1, -1)
    row6 = jnp.concatenate([_pad_lanes(dt_bias), _pad_lanes(a_log)], axis=1)
    pad = lambda r: jnp.pad(r, ((0, 0), (0, cdim - r.shape[1])))
    return jnp.concatenate([conv_w, conv_b.reshape(1, cdim), pad(row5), pad(row6), jnp.zeros((1, cdim), F32)], axis=0)


def _ssd_branch(proj3, pk, nh, w_b, pwidth):
    b, l, _ = proj3.shape
    inner, dm = w_b.shape
    hd = inner // nh
    cdim = pk.shape[1]
    q = M2_CHUNK
    c2 = lambda i, c: (0, 0)
    kern = functools.partial(_ssd_kernel, nh=nh)
    return pl.pallas_call(
        kern,
        grid=(b, l // q),
        in_specs=[pl.BlockSpec((1, q, pwidth), lambda i, c: (i, c, 0)),
                  pl.BlockSpec(pk.shape, c2),
                  pl.BlockSpec((inner, dm), c2)],
        out_specs=[pl.BlockSpec((1, q, dm), lambda i, c: (i, c, 0)),
                   pl.BlockSpec((1, nh, hd, M2_STATE), lambda i, c: (i, 0, 0, 0)),
                   pl.BlockSpec((1, M2_CONV - 1, cdim), lambda i, c: (i, 0, 0))],
        out_shape=[jax.ShapeDtypeStruct((b, l, dm), F32),
                   jax.ShapeDtypeStruct((b, nh, hd, M2_STATE), F32),
                   jax.ShapeDtypeStruct((b, M2_CONV - 1, cdim), F32)],
        scratch_shapes=[pltpu.VMEM((8 + q, cdim), F32), pltpu.VMEM((nh, hd, M2_STATE), F32)],
        compiler_params=_params("arbitrary", "arbitrary"),
        name="ssd_branch",
    )(proj3, pk, w_b.astype(BF16))


def _ssd_step_prep_kernel(xbc_ref, buf_ref, dtr_ref, cw_ref, cb_ref, dtb_ref, alog_ref,
                          xbc_o, buf_o, xdt_o, da_o, *, nh, inner):
    cdim = xbc_ref.shape[1]
    raw = xbc_ref[...]
    acc = cb_ref[...] + buf_ref[:, 0:cdim] * cw_ref[0:1, :]
    for k in range(1, M2_CONV - 1):
        acc = acc + buf_ref[:, k * cdim:(k + 1) * cdim] * cw_ref[k:k + 1, :]
    acc = acc + raw * cw_ref[M2_CONV - 1:M2_CONV, :]
    xbc = jax.nn.silu(acc)
    xbc_o[...] = xbc
    for k in range(M2_CONV - 2):
        buf_o[:, k * cdim:(k + 1) * cdim] = buf_ref[:, (k + 1) * cdim:(k + 2) * cdim]
    buf_o[:, (M2_CONV - 2) * cdim:] = raw
    dt = _softplus(dtr_ref[...] + dtb_ref[...])
    da = jnp.exp(dt * (-jnp.exp(alog_ref[...])))
    xdt_o[...] = xbc[:, :inner] * jnp.dot(dt, _head_expand(nh, inner), precision=HIGHEST,
                                         preferred_element_type=F32)
    da_o[...] = jnp.dot(da, _head_expand(nh, nh * LANES), precision=HIGHEST, preferred_element_type=F32)


def _ssd_step_state_kernel(h_ref, xdt_ref, ex_ref, ext_ref, b_ref, c_ref, da_ref, hn_ref, y_ref, *, hd):
    h = pl.program_id(0)
    xrep = jnp.dot(xdt_ref[...], ex_ref[0], precision=HIGHEST, preferred_element_type=F32)
    hn = jnp.tile(da_ref[...], (1, hd)) * h_ref[...] + xrep * jnp.tile(b_ref[...], (1, hd))
    hn_ref[...] = hn
    contrib = _bdot(hn * jnp.tile(c_ref[...], (1, hd)), ext_ref[0])

    @pl.when(h % 2 == 0)
    def _():
        y_ref[...] = contrib

    @pl.when(h % 2 == 1)
    def _():
        y_ref[...] += contrib


def _ssd_epilogue_kernel(y_ref, xbc_ref, z_ref, dsk_ref, gn_ref, wb_ref, o_ref):
    inner = y_ref.shape[1]
    y = y_ref[...] + dsk_ref[...] * xbc_ref[:, :inner]
    o_ref[...] = _gated_norm_proj(y, z_ref[...], gn_ref[...], wb_ref[...])


def _ssd_step(proj, state, conv_buf, conv_w, conv_b, dt_bias, a_log, d_skip, g_norm, w_b,
              col_xbc, col_z, col_dt):
    nb = proj.shape[0]
    nh = dt_bias.shape[0]
    inner = g_norm.shape[0]
    hd = inner // nh
    ns = M2_STATE
    cdim = conv_w.shape[1]
    dm = w_b.shape[1]
    nbuf = M2_CONV - 1
    c1 = lambda i: (0, 0)
    prep = functools.partial(_ssd_step_prep_kernel, nh=nh, inner=inner)
    xbc, new_buf, xdt, da_full = pl.pallas_call(
        prep,
        grid=(1,),
        in_specs=[pl.BlockSpec((nb, cdim), lambda i: (0, col_xbc)),
                  pl.BlockSpec((nb, nbuf * cdim), c1),
                  pl.BlockSpec((nb, LANES), lambda i: (0, col_dt)),
                  pl.BlockSpec((M2_CONV, cdim), c1), pl.BlockSpec((1, cdim), c1),
                  pl.BlockSpec((1, LANES), c1), pl.BlockSpec((1, LANES), c1)],
        out_specs=[pl.BlockSpec((nb, cdim), c1), pl.BlockSpec((nb, nbuf * cdim), c1),
                   pl.BlockSpec((nb, inner), c1), pl.BlockSpec((nb, nh * LANES), c1)],
        out_shape=[jax.ShapeDtypeStruct((nb, cdim), F32), jax.ShapeDtypeStruct((nb, nbuf * cdim), F32),
                   jax.ShapeDtypeStruct((nb, inner), F32), jax.ShapeDtypeStruct((nb, nh * LANES), F32)],
        compiler_params=_params("arbitrary"),
        name="ssd_step_prep",
    )(proj, conv_buf.reshape(nb, nbuf * cdim), proj, conv_w, conv_b.reshape(1, cdim),
      _pad_lanes(dt_bias), _pad_lanes(a_log))

    row = jnp.arange(2 * hd)[:, None]
    col = jnp.arange(hd * ns)[None, :]
    ex = jnp.stack([(row == col // ns), (row == col // ns + hd)]).astype(F32)
    ext = jnp.swapaxes(ex, 1, 2).astype(BF16)
    gcol = inner // ns
    hpg = nh // M2_GROUPS
    kern = functools.partial(_ssd_step_state_kernel, hd=hd)
    hn, y = pl.pallas_call(
        kern,
        grid=(nh,),
        in_specs=[pl.BlockSpec((nb, hd * ns), lambda h: (0, h)),
                  pl.BlockSpec((nb, 2 * hd), lambda h: (0, h // 2)),
                  pl.BlockSpec((1, 2 * hd, hd * ns), lambda h: (h % 2, 0, 0)),
                  pl.BlockSpec((1, hd * ns, 2 * hd), lambda h: (h % 2, 0, 0)),
                  pl.BlockSpec((nb, ns), lambda h: (0, gcol + h // hpg)),
                  pl.BlockSpec((nb, ns), lambda h: (0, gcol + M2_GROUPS + h // hpg)),
                  pl.BlockSpec((nb, LANES), lambda h: (0, h))],
        out_specs=[pl.BlockSpec((nb, hd * ns), lambda h: (0, h)),
                   pl.BlockSpec((nb, 2 * hd), lambda h: (0, h // 2))],
        out_shape=[jax.ShapeDtypeStruct((nb, nh * hd * ns), F32), jax.ShapeDtypeStruct((nb, inner), F32)],
        compiler_params=_params("arbitrary"),
        name="ssd_step_state",
    )(state.reshape(nb, nh * hd * ns), xdt, ex, ext, xbc, xbc, da_full)

    b_out = pl.pallas_call(
        _ssd_epilogue_kernel,
        grid=(1,),
        in_specs=[pl.BlockSpec((nb, inner), c1), pl.BlockSpec((nb, cdim), c1),
                  pl.BlockSpec((nb, inner), lambda i: (0, col_z)),
                  pl.BlockSpec((1, inner), c1), pl.BlockSpec((1, inner), c1),
                  pl.BlockSpec((inner, dm), c1)],
        out_specs=pl.BlockSpec((nb, dm), c1),
        out_shape=jax.ShapeDtypeStruct((nb, dm), F32),
        compiler_params=_params("arbitrary"),
        name="ssd_step_epilogue",
    )(y, xbc, proj, jnp.repeat(d_skip, hd).reshape(1, inner), g_norm.reshape(1, inner), w_b.astype(BF16))
    return b_out, hn.reshape(nb, nh, hd, ns), new_buf.reshape(nb, nbuf, cdim)


def _merge_kernel(x_ref, a_ref, b_ref, ga_ref, gb_ref, gt_ref, sh_ref, sc_ref, g_ref, w_ref, x1_ref, h2t_ref):
    merged = jax.nn.sigmoid(ga_ref[...]) * a_ref[...] + jax.nn.sigmoid(gb_ref[...]) * b_ref[...]
    x1 = x_ref[...] + _rows(gt_ref) * _bdot(merged, w_ref[...])
    x1_ref[...] = x1
    h2 = _rmsnorm(x1, g_ref[...]) * (1.0 + _rows(sc_ref)) + _rows(sh_ref)
    h2t_ref[...] = h2.T.astype(BF16)


def _merge(x2d, a2d, b2d, proj, mod, g2, w_out, col_ga, col_gb, tt):
    t, d = x2d.shape
    tile_of = lambda i: i
    row = lambda i: (i, 0)
    return pl.pallas_call(
        _merge_kernel,
        grid=(t // tt,),
        in_specs=[pl.BlockSpec((tt, d), row), pl.BlockSpec((tt, d), row), pl.BlockSpec((tt, d), row),
                  pl.BlockSpec((tt, d), lambda i: (i, col_ga)), pl.BlockSpec((tt, d), lambda i: (i, col_gb)),
                  mod.spec(2, tile_of), mod.spec(3, tile_of), mod.spec(4, tile_of),
                  pl.BlockSpec((1, d), lambda i: (0, 0)), pl.BlockSpec((d, d), lambda i: (0, 0))],
        out_specs=[pl.BlockSpec((tt, d), row), pl.BlockSpec((d, tt), lambda i: (0, i))],
        out_shape=[jax.ShapeDtypeStruct((t, d), F32), jax.ShapeDtypeStruct((d, t), BF16)],
        compiler_params=_params("arbitrary"),
        name="merge",
    )(x2d, a2d, b2d, proj, proj, mod.arr, mod.arr, mod.arr, g2.reshape(1, d), w_out.astype(BF16))


def _top_pairs(n):
    return [(i, j) for i in range(1, n + 1) for j in range(1, n // i + 1)]


def _rot_tile(lt, k, tt):
    p = (lt + k) % (tt // LANES)
    return slice(p * LANES, (p + 1) * LANES)


def _route_kernel(h2t_ref, wq_ref, keys_ref, r_ref, sc_ref, top_ref, cand_ref):
    c_ref, f_ref, s1_ref, e1_ref = (r_ref.at[k * PEER_HEADS:(k + 1) * PEER_HEADS] for k in range(4))
    ntop = PEER_TOPK + 1
    tt = h2t_ref.shape[1]
    qt = jnp.dot(wq_ref[...], h2t_ref[...], preferred_element_type=F32)
    dh = qt.shape[0] // (2 * PEER_HEADS)
    neg = jnp.full((1, tt), -jnp.inf, F32)

    def extract(s, ref, n):
        for r in range(n):
            m = jnp.max(s, axis=0, keepdims=True)
            ref[r:r + 1, :] = m
            if r < n - 1:
                s = jnp.where(s >= m, -jnp.inf, s)

    for hs in range(2 * PEER_HEADS):
        s = jnp.dot(keys_ref[hs], qt[hs * dh:(hs + 1) * dh].astype(BF16), preferred_element_type=F32)
        sc_ref[hs] = s
        extract(s, top_ref.at[hs], ntop)

    pairs = _top_pairs(ntop)
    npad = cand_ref.shape[0]
    for h in range(PEER_HEADS):
        ta, tb = top_ref.at[2 * h], top_ref.at[2 * h + 1]
        for idx, (i, j) in enumerate(pairs):
            cand_ref[idx:idx + 1, :] = ta[i - 1:i, :] + tb[j - 1:j, :]
        for idx in range(len(pairs), npad):
            cand_ref[idx:idx + 1, :] = neg
        extract(cand_ref[...], top_ref.at[2 * PEER_HEADS], ntop)
        best = top_ref.at[2 * PEER_HEADS]
        v1 = best[0:1, :]
        z = jnp.zeros((1, tt), F32)
        for r in range(PEER_TOPK):
            z = z + jnp.exp(best[r:r + 1, :] - v1)
        thr = 0.5 * (best[PEER_TOPK - 1:PEER_TOPK, :] + best[PEER_TOPK:PEER_TOPK + 1, :])
        s0 = sc_ref[2 * h]
        s1 = sc_ref[2 * h + 1]
        c_ref[h] = thr - s0
        f_ref[h] = jnp.exp(s0 - ta[0:1, :]) * (0.5 / z)
        e1 = jnp.exp(s1 - tb[0:1, :])
        for lt in range(tt // LANES):
            src = slice(lt * LANES, (lt + 1) * LANES)
            s1_ref[h, :, _rot_tile(lt, 2 * h, tt)] = s1[:, src]
            e1_ref[h, :, _rot_tile(lt, 2 * h + 1, tt)] = e1[:, src]


def _route(h2t, wq_t, keys, tt):
    d, t = h2t.shape
    nq = wq_t.shape[0]
    nhs = keys.shape[0]
    ntop = PEER_TOPK + 1
    npad = -(-len(_top_pairs(ntop)) // 8) * 8
    blk = pl.BlockSpec((4 * PEER_HEADS, PEER_NKEYS, tt), lambda i: (0, 0, i))
    shp = jax.ShapeDtypeStruct((4 * PEER_HEADS, PEER_NKEYS, t), F32)
    return pl.pallas_call(
        _route_kernel,
        grid=(t // tt,),
        in_specs=[pl.BlockSpec((d, tt), lambda i: (0, i)),
                  pl.BlockSpec((nq, d), lambda i: (0, 0)),
                  pl.BlockSpec(keys.shape, lambda i: (0, 0, 0))],
        out_specs=blk,
        out_shape=shp,
        scratch_shapes=[pltpu.VMEM((nhs, PEER_NKEYS, tt), F32),
                        pltpu.VMEM((nhs + 1, 24, tt), F32),
                        pltpu.VMEM((npad, tt), F32)],
        compiler_params=_params("arbitrary"),
        name="peer_route",
    )(h2t, wq_t, keys)


def _tree_sum(xs):
    while len(xs) > 1:
        xs = [xs[k] + xs[k + 1] for k in range(0, len(xs) - 1, 2)] + ([xs[-1]] if len(xs) % 2 else [])
    return xs[0]


def _peer_kernel(h2t_ref, down_ref, upt_ref, r_ref, x1_ref, pm_ref, y_ref, acc_ref, d0_ref, d1_ref, a0_ref, a1_ref,
                 *, final, ne, nsteps):
    g = pl.program_id(0)
    et, tt = d0_ref.shape
    rps = et // PEER_NKEYS
    c_ref, f_ref, s1_ref, e1_ref = (r_ref.at[k * PEER_HEADS:(k + 1) * PEER_HEADS] for k in range(4))
    dm = acc_ref.shape[0]
    t1 = jnp.clip(g - 1, 0, nsteps - 1)
    t2 = g - 2

    @pl.when(g == 0)
    def _():
        d1_ref[...] = jnp.zeros(d1_ref.shape, F32)
        a0_ref[...] = jnp.zeros(a0_ref.shape, BF16)
        a1_ref[...] = jnp.zeros(a1_ref.shape, BF16)

    @pl.when((g == 0) | ((t2 >= 0) & (t2 % ne == 0)))
    def _():
        acc_ref[...] = jnp.zeros(acc_ref.shape, F32)

    half = 8
    nlt = tt // LANES
    n_down, n_up = 2, 2

    def stages(d_new, d_cur, a_new, a_cur):
        def down_chunk(ck):
            rows = slice(ck * (et // n_down), (ck + 1) * (et // n_down))
            d_new[rows, :] = jnp.dot(down_ref[rows, :], h2t_ref[...], preferred_element_type=F32)

        def up_chunk(ck):
            rows = slice(ck * (dm // n_up), (ck + 1) * (dm // n_up))
            acc_ref[rows, :] += jnp.dot(upt_ref[rows, :], a_cur[...], preferred_element_type=F32)

        def weights_block(r, lt):
            i = (t1 % ne) * rps + r
            ls = slice(lt * LANES, (lt + 1) * LANES)
            cs = [jnp.broadcast_to(c_ref[h, pl.ds(i, 1), :][:, ls], (half, LANES)) for h in range(PEER_HEADS)]
            fs = [jnp.broadcast_to(f_ref[h, pl.ds(i, 1), :][:, ls], (half, LANES)) for h in range(PEER_HEADS)]
            for sb in range(PEER_NKEYS // (2 * half)):
                blk = []
                for hf in range(2):
                    j0 = sb * 2 * half + hf * half
                    js = slice(j0, j0 + half)
                    w = _tree_sum([jnp.where(s1_ref[h, js, _rot_tile(lt, 2 * h, tt)] >= cs[h],
                                             e1_ref[h, js, _rot_tile(lt, 2 * h + 1, tt)], 0.0) * fs[h]
                                   for h in range(PEER_HEADS)])
                    x = d_cur[r * PEER_NKEYS + j0:r * PEER_NKEYS + j0 + half, ls]
                    blk.append(x * (1.0 + lax.erf(x * (1.0 / math.sqrt(2.0)))) * w)
                a_new[r * PEER_NKEYS + sb * 2 * half:r * PEER_NKEYS + (sb + 1) * 2 * half, ls] = (
                    jnp.concatenate(blk, axis=0).astype(BF16))

        units = []
        k1 = k2 = 0
        while k1 < n_down or k2 < n_up:
            if k1 < n_down and k1 * n_up <= k2 * n_down:
                units.append((down_chunk, k1))
                k1 += 1
            else:
                units.append((up_chunk, k2))
                k2 += 1
        blocks = [(r, lt) for r in range(rps) for lt in range(nlt)]
        for k, (r, lt) in enumerate(blocks):
            for u, (fn, ck) in enumerate(units):
                if (u * len(blocks)) // len(units) == k:
                    fn(ck)
            weights_block(r, lt)

    @pl.when(g % 2 == 0)
    def _():
        stages(d0_ref, d1_ref, a1_ref, a0_ref)

    @pl.when(g % 2 == 1)
    def _():
        stages(d1_ref, d0_ref, a0_ref, a1_ref)

    @pl.when((t2 >= 0) & (t2 % ne == ne - 1))
    def _():
        pm = _rows(pm_ref)
        x2 = x1_ref[...] + pm[:, :dm] * acc_ref[...].T
        if final:
            x2 = _rmsnorm(x2, pm[:, 3 * dm:]) * (1.0 + pm[:, 2 * dm:3 * dm]) + pm[:, dm:2 * dm]
        y_ref[...] = x2


def _peer(h2t, down, up_t3, route, x1, pm, tt, final):
    d, t = h2t.shape
    ne, _, et = up_t3.shape
    nt = t // tt
    nsteps = nt * ne
    lag = lambda g, k: jnp.clip(g - k, 0, nsteps - 1)
    tok = lambda g, k: lag(g, k) // ne
    exp = lambda g, k: lag(g, k) % ne
    tile_of = lambda g: tok(g, 2)
    rblk = pl.BlockSpec((4 * PEER_HEADS, PEER_NKEYS, tt), lambda g: (0, 0, tok(g, 1)))
    return pl.pallas_call(
        functools.partial(_peer_kernel, final=final, ne=ne, nsteps=nsteps),
        grid=(nsteps + 2,),
        in_specs=[pl.BlockSpec((d, tt), lambda g: (0, tok(g, 0))),
                  pl.BlockSpec((et, d), lambda g: (exp(g, 0), 0)),
                  pl.BlockSpec((None, d, et), lambda g: (exp(g, 2), 0, 0)),
                  rblk,
                  pl.BlockSpec((tt, d), lambda g: (tok(g, 2), 0)),
                  pm.spec(0, tile_of)],
        out_specs=pl.BlockSpec((tt, d), lambda g: (tok(g, 2), 0)),
        out_shape=jax.ShapeDtypeStruct((t, d), F32),
        scratch_shapes=[pltpu.VMEM((d, tt), F32), pltpu.VMEM((et, tt), F32), pltpu.VMEM((et, tt), F32),
                        pltpu.VMEM((et, tt), BF16), pltpu.VMEM((et, tt), BF16)],
        compiler_params=_params("arbitrary"),
        name="peer_experts",
    )(h2t, down, up_t3, route, x1, pm.arr)


def _reorder_w_in(w_in, s5w, inner, cdim, nh, pwidth):
    o1 = s5w
    o2 = o1 + inner
    o3 = o2 + cdim
    o4 = o3 + nh
    pad = jnp.zeros((w_in.shape[0], pwidth - (cdim + inner + nh)), w_in.dtype)
    return jnp.concatenate([w_in[:, o2:o3], w_in[:, o1:o2], w_in[:, o3:o4], pad, w_in[:, :o1], w_in[:, o4:]],
                           axis=1).astype(BF16)


def kernel(x_prompt, x_sample, state_s5_re, state_s5_im, state_ssm, state_conv, c_prompt, c_sample, w_ada, b_ada, g_norm1, g_norm2, w_in, s5_a_re, s5_a_im, s5_log_dt, s5_b_re, s5_b_im, s5_c_re, s5_c_im, s5_d, s5_w_glu, s5_b_glu, w_branch_a, conv_w, conv_b, dt_bias, a_log, d_skip, g_ssm_norm, w_branch_b, w_out, w_query, sub_keys, expert_down, expert_up, g_final, w_ada_final, b_ada_final):
    depth = w_ada.shape[0]
    bp, lp, d = x_prompt.shape
    bs = x_sample.shape[0]
    assert x_sample.shape[1] == 1 and lp % M2_CHUNK == 0
    g5, p5 = s5_a_re.shape[1:]
    s5w = g5 * S5_GROUP
    inner = g_ssm_norm.shape[1]
    nh = dt_bias.shape[1]
    cdim = conv_w.shape[2]
    tp = bp * lp

    pwidth = -(-(cdim + inner + LANES) // s5w) * s5w
    assert (pwidth + s5w) % d == 0 and cdim % inner == 0
    col_xbc, col_z, col_dt = 0, cdim // inner, (cdim + inner) // LANES
    col_u = pwidth // s5w
    col_ga = (pwidth + s5w) // d
    col_gb = col_ga + 1

    c_all = jnp.concatenate([c_prompt, c_sample], axis=0)
    mod_fin = _cond_proj(c_all, w_ada_final, b_ada_final)
    tt_p = 512
    et = 1024
    gfb = jnp.broadcast_to(g_final.reshape(1, d), (bp + bs, d))

    ne = expert_up.shape[1] // et
    down_all = expert_down.reshape(depth * ne * et, d).astype(BF16)
    up_all = jnp.swapaxes(expert_up.reshape(depth * ne, et, d), 1, 2).astype(BF16)

    xp = x_prompt.reshape(tp, d)
    xs = x_sample.reshape(bs, d)
    outs_p, outs_s = [], []
    for l in range(depth):
        mod_all = _cond_proj(c_all, w_ada[l], b_ada[l])
        mod_p = _Mod(mod_all[:bp], d, False, lp // tt_p)
        mod_s = _Mod(mod_all[bp:], d, True, 1)
        final = l == depth - 1
        w_in_r = _reorder_w_in(w_in[l], s5w, inner, cdim, nh, pwidth)
        pm_all = jnp.concatenate([mod_all[:, 5 * d:], mod_fin, gfb], axis=1)
        pm_p = _Mod(pm_all[:bp], 4 * d, False, lp // tt_p)
        pm_s = _Mod(pm_all[bp:], 4 * d, True, 1)
        pk = _ssd_params(conv_w[l], conv_b[l], dt_bias[l], a_log[l], d_skip[l], g_ssm_norm[l])
        s5p = _s5_prep(s5_a_re[l], s5_a_im[l], s5_log_dt[l], s5_b_re[l], s5_b_im[l], s5_c_re[l], s5_c_im[l])
        d5 = s5_d[l].reshape(1, s5w)
        wq_t = w_query[l].T.astype(BF16)
        keys = sub_keys[l].reshape(2 * PEER_HEADS, PEER_NKEYS, -1).astype(BF16)
        down = down_all[l * ne * et:(l + 1) * ne * et]
        up_t3 = up_all[l * ne:(l + 1) * ne]

        proj = _inproj(xp, g_norm1[l], mod_p, w_in_r, tt_p)
        proj3 = proj.reshape(bp, lp, -1)
        zeros5 = jnp.zeros((bp, g5 * p5), F32)
        a_out, sr, si = _s5_branch(proj3, col_u, zeros5, zeros5, s5p, d5, s5_w_glu[l], s5_b_glu[l],
                                   w_branch_a[l], bp, 128)
        b_out, hl, nbuf = _ssd_branch(proj3, pk, nh, w_branch_b[l], pwidth)
        x1, h2t = _merge(xp, a_out.reshape(tp, d), b_out.reshape(tp, d), proj, mod_p, g_norm2[l], w_out[l],
                         col_ga, col_gb, tt_p)
        route = _route(h2t, wq_t, keys, tt_p)
        xp = _peer(h2t, down, up_t3, route, x1, pm_p, tt_p, final)
        outs_p.append((sr.reshape(bp, g5, p5), si.reshape(bp, g5, p5), hl, nbuf))

        proj_s = _inproj(xs, g_norm1[l], mod_s, w_in_r, bs)
        a_s, sr_s, si_s = _s5_branch(proj_s.reshape(1, bs, -1), col_u, state_s5_re[l].reshape(bs, g5 * p5),
                                     state_s5_im[l].reshape(bs, g5 * p5), s5p, d5, s5_w_glu[l], s5_b_glu[l],
                                     w_branch_a[l], bs, 1)
        b_s, hl_s, nbuf_s = _ssd_step(proj_s, state_ssm[l], state_conv[l], conv_w[l], conv_b[l], dt_bias[l],
                                      a_log[l], d_skip[l], g_ssm_norm[l], w_branch_b[l], col_xbc, col_z, col_dt)
        x1_s, h2t_s = _merge(xs, a_s.reshape(bs, d), b_s, proj_s, mod_s, g_norm2[l], w_out[l], col_ga, col_gb, bs)
        route_s = _route(h2t_s, wq_t, keys, bs)
        xs = _peer(h2t_s, down, up_t3, route_s, x1_s, pm_s, bs, final)
        outs_s.append((sr_s.reshape(bs, g5, p5), si_s.reshape(bs, g5, p5), hl_s, nbuf_s))

    stack = lambda outs, k: jnp.stack([o[k] for o in outs])
    return (xp.reshape(bp, lp, d), xs.reshape(bs, 1, d),
            stack(outs_p, 0), stack(outs_p, 1), stack(outs_p, 2), stack(outs_p, 3),
            stack(outs_s, 0), stack(outs_s, 1), stack(outs_s, 2), stack(outs_s, 3))
```

```python
import functools
import math

import jax
import jax.numpy as jnp
import numpy as np
from jax import lax
from jax.experimental import pallas as pl
from jax.experimental.pallas import tpu as pltpu

F32 = jnp.float32
BF16 = jnp.bfloat16
HIGHEST = lax.Precision.HIGHEST

LANES = 128
NORM_EPS = 1e-6
GNORM_EPS = 1e-5

S5_GROUP = 16
S5_STATE = 64
M2_HEADDIM = 64
M2_GROUPS = 4
M2_STATE = 128
M2_CONV = 4
M2_CHUNK = 128
PEER_HEADS = 8
PEER_NKEYS = 128
PEER_TOPK = 16

VMEM_LIMIT = 56 * 1024 * 1024


def _params(*sem):
    return pltpu.CompilerParams(dimension_semantics=sem, vmem_limit_bytes=VMEM_LIMIT)


def _bdot(a, b):
    return jnp.dot(a.astype(BF16), b.astype(BF16), preferred_element_type=F32)


def _rmsnorm(x, g):
    return x * lax.rsqrt(jnp.mean(x * x, axis=-1, keepdims=True) + NORM_EPS) * g


def _rows(ref):
    v = ref[...]
    return v.reshape(-1, v.shape[-1])


def _gelu(x):
    return 0.5 * x * (1.0 + lax.erf(x * (1.0 / math.sqrt(2.0))))


def _softplus(x):
    return jnp.maximum(x, 0.0) + jnp.log1p(jnp.exp(-jnp.abs(x)))


def _cond_kernel(c_ref, w_ref, b_ref, o_ref):
    sc = jax.nn.silu(c_ref[...])
    o_ref[...] = jnp.dot(sc, w_ref[...], precision=HIGHEST, preferred_element_type=F32) + b_ref[...]


def _cond_proj(c, w, b):
    r, d = c.shape
    n = w.shape[1]
    tn = 1024
    return pl.pallas_call(
        _cond_kernel,
        grid=(n // tn,),
        in_specs=[pl.BlockSpec((r, d), lambda j: (0, 0)),
                  pl.BlockSpec((d, tn), lambda j: (0, j)),
                  pl.BlockSpec((1, tn), lambda j: (0, j))],
        out_specs=pl.BlockSpec((r, tn), lambda j: (0, j)),
        out_shape=jax.ShapeDtypeStruct((r, n), F32),
        compiler_params=_params("arbitrary"),
        name="cond_proj",
    )(c, w, b.reshape(1, n))


class _Mod:
    def __init__(self, mod, d, per_token, tiles_per_batch):
        self.per_token = per_token
        self.tpb = tiles_per_batch
        self.d = d
        self.arr = mod if per_token else mod.reshape(mod.shape[0], 1, mod.shape[1])

    def spec(self, k, tile_of):
        if self.per_token:
            return pl.BlockSpec((self.arr.shape[0], self.d), lambda *g: (0, k))
        return pl.BlockSpec((1, 1, self.d), lambda *g: (tile_of(*g) // self.tpb, 0, k))


def _inproj_kernel(x_ref, g_ref, sh_ref, sc_ref, w_ref, o_ref):
    h = _rmsnorm(x_ref[...], g_ref[...]) * (1.0 + _rows(sc_ref)) + _rows(sh_ref)
    o_ref[...] = _bdot(h, w_ref[...])


def _inproj(x2d, g, mod, w_bf16, tt):
    t, d = x2d.shape
    n = w_bf16.shape[1]
    tn = n // 3
    tile_of = lambda j, i: i
    return pl.pallas_call(
        _inproj_kernel,
        grid=(n // tn, t // tt),
        in_specs=[pl.BlockSpec((tt, d), lambda j, i: (i, 0)),
                  pl.BlockSpec((1, d), lambda j, i: (0, 0)),
                  mod.spec(0, tile_of), mod.spec(1, tile_of),
                  pl.BlockSpec((d, tn), lambda j, i: (0, j))],
        out_specs=pl.BlockSpec((tt, tn), lambda j, i: (i, j)),
        out_shape=jax.ShapeDtypeStruct((t, n), F32),
        compiler_params=_params("arbitrary", "arbitrary"),
        name="inproj",
    )(x2d, g.reshape(1, d), mod.arr, mod.arr, w_bf16)


def _s5_kernel(u_ref, s0r_ref, s0i_ref, ar_ref, ai_ref, bre_ref, bim_ref, cre_ref, cim_ref, d_ref,
               wglu_ref, bglu_ref, wa_ref, o_ref, sr_ref, si_ref, s_ref, carry_ref, ubm_ref, utm_ref, *, nb, lc):
    nt = ar_ref.shape[1] // LANES
    nu = ubm_ref.shape[0]
    c = pl.program_id(0)

    u_bm = _rows(u_ref)
    for k in range(nu):
        ubm_ref[k] = u_bm[:, k * LANES:(k + 1) * LANES]

    def to_time_major(t, _):
        dst = pl.ds(pl.multiple_of(t * nb, nb), nb)
        for k in range(nu):
            utm_ref[k, dst, :] = ubm_ref[k, pl.ds(t, nb, stride=lc), :]
        return 0

    lax.fori_loop(0, lc, to_time_major, 0)
    u = jnp.concatenate([utm_ref[k] for k in range(nu)], axis=1)
    ub = u.astype(BF16)

    @pl.when(c == 0)
    def _():
        for j in range(nt):
            carry_ref[j] = s0r_ref[:, j * LANES:(j + 1) * LANES]
            carry_ref[nt + j] = s0i_ref[:, j * LANES:(j + 1) * LANES]

    gpt = LANES // S5_GROUP
    spt = LANES // S5_STATE
    for j in range(nt):
        k = (j * spt) // gpt
        uk = ub[:, k * LANES:(k + 1) * LANES]
        s_ref[j] = jnp.dot(uk, bre_ref[j], preferred_element_type=F32)
        s_ref[nt + j] = jnp.dot(uk, bim_ref[j], preferred_element_type=F32)

    def step(t, carry):
        new = []
        for j in range(nt):
            cr, ci = carry[2 * j], carry[2 * j + 1]
            ar = ar_ref[:, j * LANES:(j + 1) * LANES]
            ai = ai_ref[:, j * LANES:(j + 1) * LANES]
            idx = pl.ds(pl.multiple_of(t * nb, nb), nb)
            nr = ar * cr - ai * ci + s_ref[j, idx, :]
            ni = ar * ci + ai * cr + s_ref[nt + j, idx, :]
            s_ref[j, idx, :] = nr
            s_ref[nt + j, idx, :] = ni
            new += [nr, ni]
        return tuple(new)

    init = []
    for j in range(nt):
        init += [carry_ref[j], carry_ref[nt + j]]
    fin = lax.fori_loop(0, lc, step, tuple(init))
    for j in range(nt):
        carry_ref[j] = fin[2 * j]
        carry_ref[nt + j] = fin[2 * j + 1]
        sr_ref[:, j * LANES:(j + 1) * LANES] = fin[2 * j]
        si_ref[:, j * LANES:(j + 1) * LANES] = fin[2 * j + 1]

    tpo = gpt // spt
    ys = []
    for m in range(u.shape[1] // LANES):
        acc = d_ref[:, m * LANES:(m + 1) * LANES] * u[:, m * LANES:(m + 1) * LANES]
        for k in range(tpo):
            j = tpo * m + k
            acc = acc + jnp.dot(s_ref[j].astype(BF16), cre_ref[j], preferred_element_type=F32)
            acc = acc - jnp.dot(s_ref[nt + j].astype(BF16), cim_ref[j], preferred_element_type=F32)
        ys.append(_gelu(acc))

    for k in range(nu):
        utm_ref[k] = ys[k]

    def to_batch_major(t, _):
        src = pl.ds(pl.multiple_of(t * nb, nb), nb)
        for k in range(nu):
            ubm_ref[k, pl.ds(t, nb, stride=lc), :] = utm_ref[k, src, :]
        return 0

    lax.fori_loop(0, lc, to_batch_major, 0)
    y = jnp.concatenate([ubm_ref[k] for k in range(nu)], axis=1)
    y = y * jax.nn.sigmoid(_bdot(y, wglu_ref[...]) + bglu_ref[...])
    o_ref[...] = _bdot(y, wa_ref[...]).reshape(o_ref.shape)


def _s5_prep(a_re, a_im, log_dt, b_re, b_im, c_re, c_im):
    g, p = a_re.shape
    h = b_re.shape[-1]
    dt = jnp.exp(log_dt)[:, None]
    mag = jnp.exp(dt * a_re)
    abar_re = mag * jnp.cos(dt * a_im)
    abar_im = mag * jnp.sin(dt * a_im)
    den = a_re * a_re + a_im * a_im
    nr = abar_re - 1.0
    f_re = (nr * a_re + abar_im * a_im) / den
    f_im = (abar_im * a_re - nr * a_im) / den
    bbar_re = f_re[..., None] * b_re - f_im[..., None] * b_im
    bbar_im = f_re[..., None] * b_im + f_im[..., None] * b_re
    gpt = LANES // h
    spt = LANES // p
    nt = g // spt
    place = (np.arange(g).reshape(nt, spt, 1) % gpt == np.arange(gpt)).astype(np.float32)

    def in_blocks(bbar):
        return jnp.einsum('jsph,jsl->jlhsp', bbar.reshape(nt, spt, p, h), place).reshape(nt, LANES, LANES).astype(BF16)

    def out_blocks(cm):
        return jnp.einsum('jshp,jsl->jsplh', cm.reshape(nt, spt, h, p), place).reshape(nt, LANES, LANES).astype(BF16)

    return (abar_re.reshape(1, g * p), abar_im.reshape(1, g * p), in_blocks(bbar_re), in_blocks(bbar_im),
            out_blocks(c_re), out_blocks(c_im))


def _s5_branch(proj3, col_blk, s0r, s0i, prep, d_flat, w_glu, b_glu, w_a, nb, lc):
    abar_re, abar_im, bre, bim, cre, cim = prep
    b3, l3, _ = proj3.shape
    width = w_glu.shape[0]
    dm = w_a.shape[1]
    ns = abar_re.shape[1]
    nt = ns // LANES
    blk_b = b3
    blk_l = nb * lc // b3
    nchunk = l3 // blk_l
    const2 = lambda c: (0, 0)
    const3 = lambda c: (0, 0, 0)
    kern = functools.partial(_s5_kernel, nb=nb, lc=lc)
    return pl.pallas_call(
        kern,
        grid=(nchunk,),
        in_specs=[pl.BlockSpec((blk_b, blk_l, width), lambda c: (0, c, col_blk)),
                  pl.BlockSpec((nb, ns), const2), pl.BlockSpec((nb, ns), const2),
                  pl.BlockSpec((1, ns), const2), pl.BlockSpec((1, ns), const2),
                  pl.BlockSpec((nt, LANES, LANES), const3), pl.BlockSpec((nt, LANES, LANES), const3),
                  pl.BlockSpec((nt, LANES, LANES), const3), pl.BlockSpec((nt, LANES, LANES), const3),
                  pl.BlockSpec((1, width), const2),
                  pl.BlockSpec((width, width), const2), pl.BlockSpec((1, width), const2),
                  pl.BlockSpec((width, dm), const2)],
        out_specs=[pl.BlockSpec((blk_b, blk_l, dm), lambda c: (0, c, 0)),
                   pl.BlockSpec((nb, ns), const2), pl.BlockSpec((nb, ns), const2)],
        out_shape=[jax.ShapeDtypeStruct((b3, l3, dm), F32),
                   jax.ShapeDtypeStruct((nb, ns), F32), jax.ShapeDtypeStruct((nb, ns), F32)],
        scratch_shapes=[pltpu.VMEM((2 * nt, nb * lc, LANES), F32),
                        pltpu.VMEM((2 * nt, nb, LANES), F32),
                        pltpu.VMEM((width // LANES, nb * lc, LANES), F32),
                        pltpu.VMEM((width // LANES, nb * lc, LANES), F32)],
        compiler_params=_params("arbitrary"),
        name="s5_branch",
    )(proj3, s0r, s0i, abar_re, abar_im, bre, bim, cre, cim, d_flat,
      w_glu.astype(BF16), b_glu.reshape(1, width), w_a.astype(BF16))


def _head_expand(nh, width):
    r = lax.broadcasted_iota(jnp.int32, (LANES, width), 0)
    c = lax.broadcasted_iota(jnp.int32, (LANES, width), 1)
    return jnp.where((c // (width // nh) == r) & (r < nh), 1.0, 0.0).astype(F32)


def _split3(x):
    hi = x.astype(BF16)
    r = x - hi.astype(F32)
    mid = r.astype(BF16)
    lo = (r - mid.astype(F32)).astype(BF16)
    return hi, mid, lo


def _select_dot(x, onehot):
    e = onehot.astype(BF16)
    return sum(jnp.dot(p, e, preferred_element_type=F32) for p in _split3(x))


def _gated_norm_proj(y, z, gn, wb):
    y = y * jax.nn.silu(z)
    gw = y.shape[1] // M2_GROUPS
    parts = []
    for g in range(M2_GROUPS):
        yg = y[:, g * gw:(g + 1) * gw]
        parts.append(yg * lax.rsqrt(jnp.mean(yg * yg, axis=-1, keepdims=True) + GNORM_EPS))
    y = jnp.concatenate(parts, axis=1) * gn
    return _bdot(y, wb)


def _ssd_kernel(p_ref, pk_ref, wb_ref, o_ref, hout_ref, conv_ref, xp_ref, h_ref, *, nh):
    q = M2_CHUNK
    c = pl.program_id(1)
    inner = wb_ref.shape[0]
    cdim = xp_ref.shape[1]
    hd = inner // nh
    cw_ref = pk_ref
    cb_ref = pk_ref.at[M2_CONV:M2_CONV + 1]
    dsk_ref = pk_ref.at[M2_CONV + 1:M2_CONV + 2, 0:inner]
    gn_ref = pk_ref.at[M2_CONV + 1:M2_CONV + 2, inner:2 * inner]
    dtb_ref = pk_ref.at[M2_CONV + 2:M2_CONV + 3, 0:LANES]
    alog_ref = pk_ref.at[M2_CONV + 2:M2_CONV + 3, LANES:2 * LANES]
    hpg = nh // M2_GROUPS
    ns = M2_STATE

    @pl.when(c == 0)
    def _():
        xp_ref[0:8] = jnp.zeros((8, xp_ref.shape[1]), F32)
        h_ref[...] = jnp.zeros(h_ref.shape, F32)

    xp_ref[8:8 + q] = p_ref[0, :, 0:cdim]
    acc = cb_ref[...] + xp_ref[5:5 + q] * cw_ref[0:1, :]
    for k in range(1, M2_CONV):
        acc = acc + xp_ref[5 + k:5 + k + q] * cw_ref[k:k + 1, :]
    xbc = jax.nn.silu(acc)
    tail = xp_ref[5 + q:8 + q]
    conv_ref[0] = tail
    xp_ref[5:8] = tail

    xs = xbc[:, :inner]
    bm = xbc[:, inner:inner + M2_GROUPS * ns]
    cm = xbc[:, inner + M2_GROUPS * ns:]

    dt = _softplus(p_ref[0, :, cdim + inner:cdim + inner + LANES] + dtb_ref[...])
    da = dt * (-jnp.exp(alog_ref[...]))
    ii = lax.broadcasted_iota(jnp.int32, (q, q), 0)
    jj = lax.broadcasted_iota(jnp.int32, (q, q), 1)
    lower = ii >= jj
    tril = jnp.where(lower, 1.0, 0.0).astype(BF16)
    cs = sum(jnp.dot(tril, p, preferred_element_type=F32) for p in _split3(da))
    cst = cs.T
    cs_last = cs[q - 1:q, :]
    expand = _head_expand(nh, inner)
    dt_full = _select_dot(dt, expand)
    dst_full = _select_dot(jnp.exp(cs_last - cs), expand)
    ecs_full = _select_dot(jnp.exp(cs), expand)
    xdt = xs * dt_full
    xwt = (xdt * dst_full).T
    lane = lax.broadcasted_iota(jnp.int32, (q, 2 * hd), 1)

    ys = []
    for g in range(M2_GROUPS):
        bg = bm[:, g * ns:(g + 1) * ns]
        cg = cm[:, g * ns:(g + 1) * ns].astype(BF16)
        cb = lax.dot_general(cg, bg.astype(BF16), (((1,), (1,)), ((), ())), preferred_element_type=F32)
        hprev = h_ref[g * hpg:(g + 1) * hpg].reshape(hpg * hd, ns)
        yoff = lax.dot_general(cg, hprev.astype(BF16), (((1,), (1,)), ((), ())), preferred_element_type=F32)
        yoff = yoff * ecs_full[:, g * hpg * hd:(g + 1) * hpg * hd]
        for pr in range(hpg // 2):
            h0 = g * hpg + 2 * pr
            ms = []
            for h in (h0, h0 + 1):
                seg = cs[:, h:h + 1] - cst[h:h + 1, :]
                ms.append((cb * jnp.exp(jnp.where(lower, seg, -jnp.inf))).astype(BF16))
            xp = xdt[:, h0 * hd:(h0 + 2) * hd]
            rhs = jnp.concatenate([jnp.where(lane < hd, xp, 0.0), jnp.where(lane >= hd, xp, 0.0)], axis=0)
            ydiag = jnp.dot(jnp.concatenate(ms, axis=1), rhs.astype(BF16), preferred_element_type=F32)
            ys.append(ydiag + yoff[:, 2 * pr * hd:(2 * pr + 2) * hd])
        st = jnp.dot(xwt[g * hpg * hd:(g + 1) * hpg * hd, :].astype(BF16), bg.astype(BF16),
                     preferred_element_type=F32)
        for hh in range(hpg):
            h = g * hpg + hh
            dec = jnp.exp(cst[h:h + 1, q - 1:q])
            h_ref[h] = dec * h_ref[h] + st[hh * hd:(hh + 1) * hd]
    hout_ref[0] = h_ref[...]
    y = jnp.concatenate(ys, axis=1) + dsk_ref[...] * xs
    o_ref[0] = _gated_norm_proj(y, p_ref[0, :, cdim:cdim + inner], gn_ref[...], wb_ref[...])


def _pad_lanes(v):
    return jnp.pad(v.reshape(1, -1), ((0, 0), (0, LANES - v.shape[-1])))


def _ssd_params(conv_w, conv_b, dt_bias, a_log, d_skip, g_norm):
    cdim = conv_w.shape[1]
    inner = g_norm.shape[0]
    hd = inner // dt_bias.shape[0]
    row5 = jnp.concatenate([jnp.repeat(d_skip, hd), g_norm]).reshape(1, -1)
    row6 = jnp.concatenate([_pad_lanes(dt_bias), _pad_lanes(a_log)], axis=1)
    pad = lambda r: jnp.pad(r, ((0, 0), (0, cdim - r.shape[1])))
    return jnp.concatenate([conv_w, conv_b.reshape(1, cdim), pad(row5), pad(row6), jnp.zeros((1, cdim), F32)], axis=0)


def _ssd_branch(proj3, pk, nh, w_b, pwidth):
    b, l, _ = proj3.shape
    inner, dm = w_b.shape
    hd = inner // nh
    cdim = pk.shape[1]
    q = M2_CHUNK
    c2 = lambda i, c: (0, 0)
    kern = functools.partial(_ssd_kernel, nh=nh)
    return pl.pallas_call(
        kern,
        grid=(b, l // q),
        in_specs=[pl.BlockSpec((1, q, pwidth), lambda i, c: (i, c, 0)),
                  pl.BlockSpec(pk.shape, c2),
                  pl.BlockSpec((inner, dm), c2)],
        out_specs=[pl.BlockSpec((1, q, dm), lambda i, c: (i, c, 0)),
                   pl.BlockSpec((1, nh, hd, M2_STATE), lambda i, c: (i, 0, 0, 0)),
                   pl.BlockSpec((1, M2_CONV - 1, cdim), lambda i, c: (i, 0, 0))],
        out_shape=[jax.ShapeDtypeStruct((b, l, dm), F32),
                   jax.ShapeDtypeStruct((b, nh, hd, M2_STATE), F32),
                   jax.ShapeDtypeStruct((b, M2_CONV - 1, cdim), F32)],
        scratch_shapes=[pltpu.VMEM((8 + q, cdim), F32), pltpu.VMEM((nh, hd, M2_STATE), F32)],
        compiler_params=_params("arbitrary", "arbitrary"),
        name="ssd_branch",
    )(proj3, pk, w_b.astype(BF16))


def _ssd_step_prep_kernel(xbc_ref, buf_ref, dtr_ref, cw_ref, cb_ref, dtb_ref, alog_ref,
                          xbc_o, buf_o, xdt_o, da_o, *, nh, inner):
    cdim = xbc_ref.shape[1]
    raw = xbc_ref[...]
    acc = cb_ref[...] + buf_ref[:, 0:cdim] * cw_ref[0:1, :]
    for k in range(1, M2_CONV - 1):
        acc = acc + buf_ref[:, k * cdim:(k + 1) * cdim] * cw_ref[k:k + 1, :]
    acc = acc + raw * cw_ref[M2_CONV - 1:M2_CONV, :]
    xbc = jax.nn.silu(acc)
    xbc_o[...] = xbc
    for k in range(M2_CONV - 2):
        buf_o[:, k * cdim:(k + 1) * cdim] = buf_ref[:, (k + 1) * cdim:(k + 2) * cdim]
    buf_o[:, (M2_CONV - 2) * cdim:] = raw
    dt = _softplus(dtr_ref[...] + dtb_ref[...])
    da = jnp.exp(dt * (-jnp.exp(alog_ref[...])))
    xdt_o[...] = xbc[:, :inner] * jnp.dot(dt, _head_expand(nh, inner), precision=HIGHEST,
                                         preferred_element_type=F32)
    da_o[...] = jnp.dot(da, _head_expand(nh, nh * LANES), precision=HIGHEST, preferred_element_type=F32)


def _ssd_step_state_kernel(h_ref, xdt_ref, ex_ref, ext_ref, b_ref, c_ref, da_ref, hn_ref, y_ref, *, hd):
    h = pl.program_id(0)
    xrep = jnp.dot(xdt_ref[...], ex_ref[0], precision=HIGHEST, preferred_element_type=F32)
    hn = jnp.tile(da_ref[...], (1, hd)) * h_ref[...] + xrep * jnp.tile(b_ref[...], (1, hd))
    hn_ref[...] = hn
    contrib = _bdot(hn * jnp.tile(c_ref[...], (1, hd)), ext_ref[0])

    @pl.when(h % 2 == 0)
    def _():
        y_ref[...] = contrib

    @pl.when(h % 2 == 1)
    def _():
        y_ref[...] += contrib


def _ssd_epilogue_kernel(y_ref, xbc_ref, z_ref, dsk_ref, gn_ref, wb_ref, o_ref):
    inner = y_ref.shape[1]
    y = y_ref[...] + dsk_ref[...] * xbc_ref[:, :inner]
    o_ref[...] = _gated_norm_proj(y, z_ref[...], gn_ref[...], wb_ref[...])


def _ssd_step(proj, state, conv_buf, conv_w, conv_b, dt_bias, a_log, d_skip, g_norm, w_b,
              col_xbc, col_z, col_dt):
    nb = proj.shape[0]
    nh = dt_bias.shape[0]
    inner = g_norm.shape[0]
    hd = inner // nh
    ns = M2_STATE
    cdim = conv_w.shape[1]
    dm = w_b.shape[1]
    nbuf = M2_CONV - 1
    c1 = lambda i: (0, 0)
    prep = functools.partial(_ssd_step_prep_kernel, nh=nh, inner=inner)
    xbc, new_buf, xdt, da_full = pl.pallas_call(
        prep,
        grid=(1,),
        in_specs=[pl.BlockSpec((nb, cdim), lambda i: (0, col_xbc)),
                  pl.BlockSpec((nb, nbuf * cdim), c1),
                  pl.BlockSpec((nb, LANES), lambda i: (0, col_dt)),
                  pl.BlockSpec((M2_CONV, cdim), c1), pl.BlockSpec((1, cdim), c1),
                  pl.BlockSpec((1, LANES), c1), pl.BlockSpec((1, LANES), c1)],
        out_specs=[pl.BlockSpec((nb, cdim), c1), pl.BlockSpec((nb, nbuf * cdim), c1),
                   pl.BlockSpec((nb, inner), c1), pl.BlockSpec((nb, nh * LANES), c1)],
        out_shape=[jax.ShapeDtypeStruct((nb, cdim), F32), jax.ShapeDtypeStruct((nb, nbuf * cdim), F32),
                   jax.ShapeDtypeStruct((nb, inner), F32), jax.ShapeDtypeStruct((nb, nh * LANES), F32)],
        compiler_params=_params("arbitrary"),
        name="ssd_step_prep",
    )(proj, conv_buf.reshape(nb, nbuf * cdim), proj, conv_w, conv_b.reshape(1, cdim),
      _pad_lanes(dt_bias), _pad_lanes(a_log))

    row = jnp.arange(2 * hd)[:, None]
    col = jnp.arange(hd * ns)[None, :]
    ex = jnp.stack([(row == col // ns), (row == col // ns + hd)]).astype(F32)
    ext = jnp.swapaxes(ex, 1, 2).astype(BF16)
    gcol = inner // ns
    hpg = nh // M2_GROUPS
    kern = functools.partial(_ssd_step_state_kernel, hd=hd)
    hn, y = pl.pallas_call(
        kern,
        grid=(nh,),
        in_specs=[pl.BlockSpec((nb, hd * ns), lambda h: (0, h)),
                  pl.BlockSpec((nb, 2 * hd), lambda h: (0, h // 2)),
                  pl.BlockSpec((1, 2 * hd, hd * ns), lambda h: (h % 2, 0, 0)),
                  pl.BlockSpec((1, hd * ns, 2 * hd), lambda h: (h % 2, 0, 0)),
                  pl.BlockSpec((nb, ns), lambda h: (0, gcol + h // hpg)),
                  pl.BlockSpec((nb, ns), lambda h: (0, gcol + M2_GROUPS + h // hpg)),
                  pl.BlockSpec((nb, LANES), lambda h: (0, h))],
        out_specs=[pl.BlockSpec((nb, hd * ns), lambda h: (0, h)),
                   pl.BlockSpec((nb, 2 * hd), lambda h: (0, h // 2))],
        out_shape=[jax.ShapeDtypeStruct((nb, nh * hd * ns), F32), jax.ShapeDtypeStruct((nb, inner), F32)],
        compiler_params=_params("arbitrary"),
        name="ssd_step_state",
    )(state.reshape(nb, nh * hd * ns), xdt, ex, ext, xbc, xbc, da_full)

    b_out = pl.pallas_call(
        _ssd_epilogue_kernel,
        grid=(1,),
        in_specs=[pl.BlockSpec((nb, inner), c1), pl.BlockSpec((nb, cdim), c1),
                  pl.BlockSpec((nb, inner), lambda i: (0, col_z)),
                  pl.BlockSpec((1, inner), c1), pl.BlockSpec((1, inner), c1),
                  pl.BlockSpec((inner, dm), c1)],
        out_specs=pl.BlockSpec((nb, dm), c1),
        out_shape=jax.ShapeDtypeStruct((nb, dm), F32),
        compiler_params=_params("arbitrary"),
        name="ssd_step_epilogue",
    )(y, xbc, proj, jnp.repeat(d_skip, hd).reshape(1, inner), g_norm.reshape(1, inner), w_b.astype(BF16))
    return b_out, hn.reshape(nb, nh, hd, ns), new_buf.reshape(nb, nbuf, cdim)


def _merge_kernel(x_ref, a_ref, b_ref, ga_ref, gb_ref, gt_ref, sh_ref, sc_ref, g_ref, w_ref, x1_ref, h2t_ref):
    merged = jax.nn.sigmoid(ga_ref[...]) * a_ref[...] + jax.nn.sigmoid(gb_ref[...]) * b_ref[...]
    x1 = x_ref[...] + _rows(gt_ref) * _bdot(merged, w_ref[...])
    x1_ref[...] = x1
    h2 = _rmsnorm(x1, g_ref[...]) * (1.0 + _rows(sc_ref)) + _rows(sh_ref)
    h2t_ref[...] = h2.T.astype(BF16)


def _merge(x2d, a2d, b2d, proj, mod, g2, w_out, col_ga, col_gb, tt):
    t, d = x2d.shape
    tile_of = lambda i: i
    row = lambda i: (i, 0)
    return pl.pallas_call(
        _merge_kernel,
        grid=(t // tt,),
        in_specs=[pl.BlockSpec((tt, d), row), pl.BlockSpec((tt, d), row), pl.BlockSpec((tt, d), row),
                  pl.BlockSpec((tt, d), lambda i: (i, col_ga)), pl.BlockSpec((tt, d), lambda i: (i, col_gb)),
                  mod.spec(2, tile_of), mod.spec(3, tile_of), mod.spec(4, tile_of),
                  pl.BlockSpec((1, d), lambda i: (0, 0)), pl.BlockSpec((d, d), lambda i: (0, 0))],
        out_specs=[pl.BlockSpec((tt, d), row), pl.BlockSpec((d, tt), lambda i: (0, i))],
        out_shape=[jax.ShapeDtypeStruct((t, d), F32), jax.ShapeDtypeStruct((d, t), BF16)],
        compiler_params=_params("arbitrary"),
        name="merge",
    )(x2d, a2d, b2d, proj, proj, mod.arr, mod.arr, mod.arr, g2.reshape(1, d), w_out.astype(BF16))


def _top_pairs(n):
    return [(i, j) for i in range(1, n + 1) for j in range(1, n // i + 1)]


def _rot_tile(lt, k, tt):
    p = (lt + k) % (tt // LANES)
    return slice(p * LANES, (p + 1) * LANES)


def _route_kernel(h2t_ref, wq_ref, keys_ref, r_ref, sc_ref, top_ref, cand_ref):
    c_ref, f_ref, s1_ref, e1_ref = (r_ref.at[k * PEER_HEADS:(k + 1) * PEER_HEADS] for k in range(4))
    ntop = PEER_TOPK + 1
    tt = h2t_ref.shape[1]
    qt = jnp.dot(wq_ref[...], h2t_ref[...], preferred_element_type=F32)
    dh = qt.shape[0] // (2 * PEER_HEADS)
    neg = jnp.full((1, tt), -jnp.inf, F32)

    def extract(s, ref, n):
        for r in range(n):
            m = jnp.max(s, axis=0, keepdims=True)
            ref[r:r + 1, :] = m
            if r < n - 1:
                s = jnp.where(s >= m, -jnp.inf, s)

    for hs in range(2 * PEER_HEADS):
        s = jnp.dot(keys_ref[hs], qt[hs * dh:(hs + 1) * dh].astype(BF16), preferred_element_type=F32)
        sc_ref[hs] = s
        extract(s, top_ref.at[hs], ntop)

    pairs = _top_pairs(ntop)
    npad = cand_ref.shape[0]
    for h in range(PEER_HEADS):
        ta, tb = top_ref.at[2 * h], top_ref.at[2 * h + 1]
        for idx, (i, j) in enumerate(pairs):
            cand_ref[idx:idx + 1, :] = ta[i - 1:i, :] + tb[j - 1:j, :]
        for idx in range(len(pairs), npad):
            cand_ref[idx:idx + 1, :] = neg
        extract(cand_ref[...], top_ref.at[2 * PEER_HEADS], ntop)
        best = top_ref.at[2 * PEER_HEADS]
        v1 = best[0:1, :]
        z = jnp.zeros((1, tt), F32)
        for r in range(PEER_TOPK):
            z = z + jnp.exp(best[r:r + 1, :] - v1)
        thr = 0.5 * (best[PEER_TOPK - 1:PEER_TOPK, :] + best[PEER_TOPK:PEER_TOPK + 1, :])
        s0 = sc_ref[2 * h]
        s1 = sc_ref[2 * h + 1]
        c_ref[h] = thr - s0
        f_ref[h] = jnp.exp(s0 - ta[0:1, :]) * (0.5 / z)
        e1 = jnp.exp(s1 - tb[0:1, :])
        for lt in range(tt // LANES):
            src = slice(lt * LANES, (lt + 1) * LANES)
            s1_ref[h, :, _rot_tile(lt, 2 * h, tt)] = s1[:, src]
            e1_ref[h, :, _rot_tile(lt, 2 * h + 1, tt)] = e1[:, src]


def _route(h2t, wq_t, keys, tt):
    d, t = h2t.shape
    nq = wq_t.shape[0]
    nhs = keys.shape[0]
    ntop = PEER_TOPK + 1
    npad = -(-len(_top_pairs(ntop)) // 8) * 8
    blk = pl.BlockSpec((4 * PEER_HEADS, PEER_NKEYS, tt), lambda i: (0, 0, i))
    shp = jax.ShapeDtypeStruct((4 * PEER_HEADS, PEER_NKEYS, t), F32)
    return pl.pallas_call(
        _route_kernel,
        grid=(t // tt,),
        in_specs=[pl.BlockSpec((d, tt), lambda i: (0, i)),
                  pl.BlockSpec((nq, d), lambda i: (0, 0)),
                  pl.BlockSpec(keys.shape, lambda i: (0, 0, 0))],
        out_specs=blk,
        out_shape=shp,
        scratch_shapes=[pltpu.VMEM((nhs, PEER_NKEYS, tt), F32),
                        pltpu.VMEM((nhs + 1, 24, tt), F32),
                        pltpu.VMEM((npad, tt), F32)],
        compiler_params=_params("arbitrary"),
        name="peer_route",
    )(h2t, wq_t, keys)


def _tree_sum(xs):
    while len(xs) > 1:
        xs = [xs[k] + xs[k + 1] for k in range(0, len(xs) - 1, 2)] + ([xs[-1]] if len(xs) % 2 else [])
    return xs[0]


def _peer_kernel(h2t_ref, down_ref, upt_ref, r_ref, x1_ref, pm_ref, y_ref, acc_ref, d0_ref, d1_ref, a0_ref, a1_ref,
                 *, final, ne, nsteps):
    g = pl.program_id(0)
    et, tt = d0_ref.shape
    rps = et // PEER_NKEYS
    c_ref, f_ref, s1_ref, e1_ref = (r_ref.at[k * PEER_HEADS:(k + 1) * PEER_HEADS] for k in range(4))
    dm = acc_ref.shape[0]
    t1 = jnp.clip(g - 1, 0, nsteps - 1)
    t2 = g - 2

    @pl.when(g == 0)
    def _():
        d1_ref[...] = jnp.zeros(d1_ref.shape, F32)
        a0_ref[...] = jnp.zeros(a0_ref.shape, BF16)
        a1_ref[...] = jnp.zeros(a1_ref.shape, BF16)

    @pl.when((g == 0) | ((t2 >= 0) & (t2 % ne == 0)))
    def _():
        acc_ref[...] = jnp.zeros(acc_ref.shape, F32)

    half = 8
    nlt = tt // LANES
    n_down, n_up = 2, 2

    def stages(d_new, d_cur, a_new, a_cur):
        def down_chunk(ck):
            rows = slice(ck * (et // n_down), (ck + 1) * (et // n_down))
            d_new[rows, :] = jnp.dot(down_ref[rows, :], h2t_ref[...], preferred_element_type=F32)

        def up_chunk(ck):
            rows = slice(ck * (dm // n_up), (ck + 1) * (dm // n_up))
            acc_ref[rows, :] += jnp.dot(upt_ref[rows, :], a_cur[...], preferred_element_type=F32)

        def weights_block(r, lt):
            i = (t1 % ne) * rps + r
            ls = slice(lt * LANES, (lt + 1) * LANES)
            cs = [jnp.broadcast_to(c_ref[h, pl.ds(i, 1), :][:, ls], (half, LANES)) for h in range(PEER_HEADS)]
            fs = [jnp.broadcast_to(f_ref[h, pl.ds(i, 1), :][:, ls], (half, LANES)) for h in range(PEER_HEADS)]
            for sb in range(PEER_NKEYS // (2 * half)):
                blk = []
                for hf in range(2):
                    j0 = sb * 2 * half + hf * half
                    js = slice(j0, j0 + half)
                    w = _tree_sum([jnp.where(s1_ref[h, js, _rot_tile(lt, 2 * h, tt)] >= cs[h],
                                             e1_ref[h, js, _rot_tile(lt, 2 * h + 1, tt)], 0.0) * fs[h]
                                   for h in range(PEER_HEADS)])
                    x = d_cur[r * PEER_NKEYS + j0:r * PEER_NKEYS + j0 + half, ls]
                    blk.append(x * (1.0 + lax.erf(x * (1.0 / math.sqrt(2.0)))) * w)
                a_new[r * PEER_NKEYS + sb * 2 * half:r * PEER_NKEYS + (sb + 1) * 2 * half, ls] = (
                    jnp.concatenate(blk, axis=0).astype(BF16))

        units = []
        k1 = k2 = 0
        while k1 < n_down or k2 < n_up:
            if k1 < n_down and k1 * n_up <= k2 * n_down:
                units.append((down_chunk, k1))
                k1 += 1
            else:
                units.append((up_chunk, k2))
                k2 += 1
        blocks = [(r, lt) for r in range(rps) for lt in range(nlt)]
        for k, (r, lt) in enumerate(blocks):
            for u, (fn, ck) in enumerate(units):
                if (u * len(blocks)) // len(units) == k:
                    fn(ck)
            weights_block(r, lt)

    @pl.when(g % 2 == 0)
    def _():
        stages(d0_ref, d1_ref, a1_ref, a0_ref)

    @pl.when(g % 2 == 1)
    def _():
        stages(d1_ref, d0_ref, a0_ref, a1_ref)

    @pl.when((t2 >= 0) & (t2 % ne == ne - 1))
    def _():
        pm = _rows(pm_ref)
        x2 = x1_ref[...] + pm[:, :dm] * acc_ref[...].T
        if final:
            x2 = _rmsnorm(x2, pm[:, 3 * dm:]) * (1.0 + pm[:, 2 * dm:3 * dm]) + pm[:, dm:2 * dm]
        y_ref[...] = x2


def _peer(h2t, down, up_t3, route, x1, pm, tt, final):
    d, t = h2t.shape
    ne, _, et = up_t3.shape
    nt = t // tt
    nsteps = nt * ne
    lag = lambda g, k: jnp.clip(g - k, 0, nsteps - 1)
    tok = lambda g, k: lag(g, k) // ne
    exp = lambda g, k: lag(g, k) % ne
    tile_of = lambda g: tok(g, 2)
    rblk = pl.BlockSpec((4 * PEER_HEADS, PEER_NKEYS, tt), lambda g: (0, 0, tok(g, 1)))
    return pl.pallas_call(
        functools.partial(_peer_kernel, final=final, ne=ne, nsteps=nsteps),
        grid=(nsteps + 2,),
        in_specs=[pl.BlockSpec((d, tt), lambda g: (0, tok(g, 0))),
                  pl.BlockSpec((et, d), lambda g: (exp(g, 0), 0)),
                  pl.BlockSpec((None, d, et), lambda g: (exp(g, 2), 0, 0)),
                  rblk,
                  pl.BlockSpec((tt, d), lambda g: (tok(g, 2), 0)),
                  pm.spec(0, tile_of)],
        out_specs=pl.BlockSpec((tt, d), lambda g: (tok(g, 2), 0)),
        out_shape=jax.ShapeDtypeStruct((t, d), F32),
        scratch_shapes=[pltpu.VMEM((d, tt), F32), pltpu.VMEM((et, tt), F32), pltpu.VMEM((et, tt), F32),
                        pltpu.VMEM((et, tt), BF16), pltpu.VMEM((et, tt), BF16)],
        compiler_params=_params("arbitrary"),
        name="peer_experts",
    )(h2t, down, up_t3, route, x1, pm.arr)


def _reorder_w_in(w_in, s5w, inner, cdim, nh, pwidth):
    o1 = s5w
    o2 = o1 + inner
    o3 = o2 + cdim
    o4 = o3 + nh
    pad = jnp.zeros((w_in.shape[0], pwidth - (cdim + inner + nh)), w_in.dtype)
    return jnp.concatenate([w_in[:, o2:o3], w_in[:, o1:o2], w_in[:, o3:o4], pad, w_in[:, :o1], w_in[:, o4:]],
                           axis=1).astype(BF16)


def kernel(x_prompt, x_sample, state_s5_re, state_s5_im, state_ssm, state_conv, c_prompt, c_sample, w_ada, b_ada, g_norm1, g_norm2, w_in, s5_a_re, s5_a_im, s5_log_dt, s5_b_re, s5_b_im, s5_c_re, s5_c_im, s5_d, s5_w_glu, s5_b_glu, w_branch_a, conv_w, conv_b, dt_bias, a_log, d_skip, g_ssm_norm, w_branch_b, w_out, w_query, sub_keys, expert_down, expert_up, g_final, w_ada_final, b_ada_final):
    depth = w_ada.shape[0]
    bp, lp, d = x_prompt.shape
    bs = x_sample.shape[0]
    assert x_sample.shape[1] == 1 and lp % M2_CHUNK == 0
    g5, p5 = s5_a_re.shape[1:]
    s5w = g5 * S5_GROUP
    inner = g_ssm_norm.shape[1]
    nh = dt_bias.shape[1]
    cdim = conv_w.shape[2]
    tp = bp * lp

    pwidth = -(-(cdim + inner + LANES) // s5w) * s5w
    assert (pwidth + s5w) % d == 0 and cdim % inner == 0
    col_xbc, col_z, col_dt = 0, cdim // inner, (cdim + inner) // LANES
    col_u = pwidth // s5w
    col_ga = (pwidth + s5w) // d
    col_gb = col_ga + 1

    c_all = jnp.concatenate([c_prompt, c_sample], axis=0)
    mod_fin = _cond_proj(c_all, w_ada_final, b_ada_final)
    tt_p = 512
    et = 1024
    gfb = jnp.broadcast_to(g_final.reshape(1, d), (bp + bs, d))

    ne = expert_up.shape[1] // et
    down_all = expert_down.reshape(depth * ne * et, d).astype(BF16)
    up_all = jnp.swapaxes(expert_up.reshape(depth * ne, et, d), 1, 2).astype(BF16)

    xp = x_prompt.reshape(tp, d)
    xs = x_sample.reshape(bs, d)
    outs_p, outs_s = [], []
    for l in range(depth):
        mod_all = _cond_proj(c_all, w_ada[l], b_ada[l])
        mod_p = _Mod(mod_all[:bp], d, False, lp // tt_p)
        mod_s = _Mod(mod_all[bp:], d, True, 1)
        final = l == depth - 1
        w_in_r = _reorder_w_in(w_in[l], s5w, inner, cdim, nh, pwidth)
        pm_all = jnp.concatenate([mod_all[:, 5 * d:], mod_fin, gfb], axis=1)
        pm_p = _Mod(pm_all[:bp], 4 * d, False, lp // tt_p)
        pm_s = _Mod(pm_all[bp:], 4 * d, True, 1)
        pk = _ssd_params(conv_w[l], conv_b[l], dt_bias[l], a_log[l], d_skip[l], g_ssm_norm[l])
        s5p = _s5_prep(s5_a_re[l], s5_a_im[l], s5_log_dt[l], s5_b_re[l], s5_b_im[l], s5_c_re[l], s5_c_im[l])
        d5 = s5_d[l].reshape(1, s5w)
        wq_t = w_query[l].T.astype(BF16)
        keys = sub_keys[l].reshape(2 * PEER_HEADS, PEER_NKEYS, -1).astype(BF16)
        down = down_all[l * ne * et:(l + 1) * ne * et]
        up_t3 = up_all[l * ne:(l + 1) * ne]

        proj = _inproj(xp, g_norm1[l], mod_p, w_in_r, tt_p)
        proj3 = proj.reshape(bp, lp, -1)
        zeros5 = jnp.zeros((bp, g5 * p5), F32)
        a_out, sr, si = _s5_branch(proj3, col_u, zeros5, zeros5, s5p, d5, s5_w_glu[l], s5_b_glu[l],
                                   w_branch_a[l], bp, 128)
        b_out, hl, nbuf = _ssd_branch(proj3, pk, nh, w_branch_b[l], pwidth)
        x1, h2t = _merge(xp, a_out.reshape(tp, d), b_out.reshape(tp, d), proj, mod_p, g_norm2[l], w_out[l],
                         col_ga, col_gb, tt_p)
        route = _route(h2t, wq_t, keys, tt_p)
        xp = _peer(h2t, down, up_t3, route, x1, pm_p, tt_p, final)
        outs_p.append((sr.reshape(bp, g5, p5), si.reshape(bp, g5, p5), hl, nbuf))

        proj_s = _inproj(xs, g_norm1[l], mod_s, w_in_r, bs)
        a_s, sr_s, si_s = _s5_branch(proj_s.reshape(1, bs, -1), col_u, state_s5_re[l].reshape(bs, g5 * p5),
                                     state_s5_im[l].reshape(bs, g5 * p5), s5p, d5, s5_w_glu[l], s5_b_glu[l],
                                     w_branch_a[l], bs, 1)
        b_s, hl_s, nbuf_s = _ssd_step(proj_s, state_ssm[l], state_conv[l], conv_w[l], conv_b[l], dt_bias[l],
                                      a_log[l], d_skip[l], g_ssm_norm[l], w_branch_b[l], col_xbc, col_z, col_dt)
        x1_s, h2t_s = _merge(xs, a_s.reshape(bs, d), b_s, proj_s, mod_s, g_norm2[l], w_out[l], col_ga, col_gb, bs)
        route_s = _route(h2t_s, wq_t, keys, bs)
        xs = _peer(h2t_s, down, up_t3, route_s, x1_s, pm_s, bs, final)
        outs_s.append((sr_s.reshape(bs, g5, p5), si_s.reshape(bs, g5, p5), hl_s, nbuf_s))

    stack = lambda outs, k: jnp.stack([o[k] for o in outs])
    return (xp.reshape(bp, lp, d), xs.reshape(bs, 1, d),
            stack(outs_p, 0), stack(outs_p, 1), stack(outs_p, 2), stack(outs_p, 3),
            stack(outs_s, 0), stack(outs_s, 1), stack(outs_s, 2), stack(outs_s, 3))
```
